```python
import math
import jax, jax.numpy as jnp
from jax import lax
import numpy as np

D_MODEL = 1024
BATCH = 4
SEQ = 8192
DEPTH = 2

N_MIXERS = 2
N_HEADS = 8
HEAD_DIM = D_MODEL // N_HEADS
ATTN_WIDTH = N_HEADS * HEAD_DIM
BLOCK = 256
TOPK_BLOCKS = 3
QUERY_CHUNK = 32
CONV_WIDTH = 3
D_FF = int(math.ceil((8 * D_MODEL / 3) / 256) * 256)
N_ATTN_LAYERS = (DEPTH + 1) // 2
N_CONV_LAYERS = DEPTH // 2
EPS = 1e-6

kernel_name = "hybrid_moba_shortconv_adaln"


def rms_norm(x, gain):
    xf = x.astype(jnp.float32)
    y = xf * lax.rsqrt(jnp.mean(xf * xf, axis=-1, keepdims=True) + EPS)
    return (y * gain.astype(jnp.float32)).astype(x.dtype)


def modulate(x, gain, shift, scale):
    return rms_norm(x, gain) * (1 + scale[:, None, :]) + shift[:, None, :]


def moba_attention(h, w_qkv, w_o, q_gain, k_gain):
    B, S, _ = h.shape
    qkv = (h @ w_qkv).reshape(B, S, 3, N_HEADS, HEAD_DIM).transpose(2, 0, 3, 1, 4)
    q = rms_norm(qkv[0], q_gain) * (HEAD_DIM ** -0.5)
    k = rms_norm(qkv[1], k_gain)
    v = qkv[2]
    n_blocks = -(-S // BLOCK)
    pad = n_blocks * BLOCK - S
    kp = jnp.pad(k, ((0, 0), (0, 0), (0, pad), (0, 0)))
    vp = jnp.pad(v, ((0, 0), (0, 0), (0, pad), (0, 0)))
    k_blocks = kp.reshape(B, N_HEADS, n_blocks, BLOCK, HEAD_DIM)
    v_blocks = vp.reshape(B, N_HEADS, n_blocks, BLOCK, HEAD_DIM)
    k_mean = jnp.mean(k_blocks.astype(jnp.float32), axis=3).astype(k.dtype)
    n_sel = min(TOPK_BLOCKS, n_blocks)
    bi = jnp.arange(B)[:, None, None, None]
    hi = jnp.arange(N_HEADS)[None, :, None, None]
    blk_ids = jnp.arange(n_blocks)

    def chunk(ci):
        q0 = ci * QUERY_CHUNK
        nb = q0 // BLOCK
        qc = lax.dynamic_slice_in_dim(q, q0, QUERY_CHUNK, axis=2)
        k_own = lax.dynamic_slice_in_dim(kp, nb * BLOCK, BLOCK, axis=2)
        v_own = lax.dynamic_slice_in_dim(vp, nb * BLOCK, BLOCK, axis=2)
        route = jnp.einsum('bhqd,bhnd->bhqn', qc, k_mean).astype(jnp.float32)
        route = jnp.where(blk_ids < nb, route, -jnp.inf)
        _, idx = lax.top_k(route, n_sel)
        k_sel = k_blocks[bi, hi, idx]
        v_sel = v_blocks[bi, hi, idx]
        qpos = q0 + jnp.arange(QUERY_CHUNK)
        kpos = nb * BLOCK + jnp.arange(BLOCK)
        s_own = jnp.einsum('bhqd,bhkd->bhqk', qc, k_own).astype(jnp.float32)
        s_own = jnp.where(kpos[None, :] <= qpos[:, None], s_own, -jnp.inf)
        s_sel = jnp.einsum('bhqd,bhqjkd->bhqjk', qc, k_sel).astype(jnp.float32)
        slot_ok = jnp.arange(n_sel) < nb
        s_sel = jnp.where(slot_ok[:, None], s_sel, -jnp.inf)
        s_sel = s_sel.reshape(B, N_HEADS, QUERY_CHUNK, n_sel * BLOCK)
        p = jax.nn.softmax(jnp.concatenate([s_own, s_sel], axis=-1), axis=-1).astype(v.dtype)
        p_own = p[..., :BLOCK]
        p_sel = p[..., BLOCK:].reshape(B, N_HEADS, QUERY_CHUNK, n_sel, BLOCK)
        return (jnp.einsum('bhqk,bhkd->bhqd', p_own, v_own)
                + jnp.einsum('bhqjk,bhqjkd->bhqd', p_sel, v_sel))

    o = lax.map(chunk, jnp.arange(S // QUERY_CHUNK))
    o = o.transpose(1, 0, 3, 2, 4).reshape(B, S, ATTN_WIDTH)
    return o @ w_o


def short_conv(h, w_in, conv_w, w_out):
    S = h.shape[1]
    b_gate, c_gate, u = jnp.split(h @ w_in, 3, axis=-1)
    u = c_gate * u
    up = jnp.pad(u, ((0, 0), (CONV_WIDTH - 1, 0), (0, 0)))
    y = sum(conv_w[j] * up[:, j:j + S] for j in range(CONV_WIDTH))
    return (b_gate * y) @ w_out


def swiglu(h, w_gate_up, w_down):
    g, u = jnp.split(h @ w_gate_up, 2, axis=-1)
    return (jax.nn.silu(g) * u) @ w_down


def setup_inputs(seed: int = 0) -> dict:
    key = jax.random.key(seed)
    ks = jax.random.split(key, 16)
    f32 = jnp.float32
    nrm = lambda k, shape, s: (jax.random.normal(k, shape, f32) * s)
    return {
        "x": nrm(ks[0], (BATCH, SEQ, D_MODEL), 1.0),
        "c": nrm(ks[1], (BATCH, D_MODEL), 1.0),
        "w_ada": nrm(ks[2], (DEPTH, D_MODEL, 6 * D_MODEL), 0.5 * D_MODEL ** -0.5),
        "b_ada": nrm(ks[3], (DEPTH, 6 * D_MODEL), 0.02),
        "norm_mix": 1.0 + nrm(ks[4], (DEPTH, D_MODEL), 0.02),
        "norm_ffn": 1.0 + nrm(ks[5], (DEPTH, D_MODEL), 0.02),
        "w_qkv": nrm(ks[6], (N_ATTN_LAYERS, D_MODEL, 3 * ATTN_WIDTH), D_MODEL ** -0.5),
        "w_o": nrm(ks[7], (N_ATTN_LAYERS, ATTN_WIDTH, D_MODEL), ATTN_WIDTH ** -0.5),
        "q_gain": 1.0 + nrm(ks[8], (N_ATTN_LAYERS, HEAD_DIM), 0.02),
        "k_gain": 1.0 + nrm(ks[9], (N_ATTN_LAYERS, HEAD_DIM), 0.02),
        "w_in": nrm(ks[10], (N_CONV_LAYERS, D_MODEL, 3 * D_MODEL), D_MODEL ** -0.5),
        "conv_w": nrm(ks[11], (N_CONV_LAYERS, CONV_WIDTH, D_MODEL), CONV_WIDTH ** -0.5),
        "w_out": nrm(ks[12], (N_CONV_LAYERS, D_MODEL, D_MODEL), D_MODEL ** -0.5),
        "w_gate_up": nrm(ks[13], (DEPTH, D_MODEL, 2 * D_FF), D_MODEL ** -0.5),
        "w_down": nrm(ks[14], (DEPTH, D_FF, D_MODEL), D_FF ** -0.5),
    }


def reference(x, c, w_ada, b_ada, norm_mix, norm_ffn, w_qkv, w_o, q_gain, k_gain,
              w_in, conv_w, w_out, w_gate_up, w_down):
    sc = jax.nn.silu(c)
    for i in range(DEPTH):
        mod = sc @ w_ada[i] + b_ada[i]
        sh_m, sc_m, g_m, sh_f, sc_f, g_f = jnp.split(mod, 6, axis=-1)
        h = modulate(x, norm_mix[i], sh_m, sc_m)
        j = i // N_MIXERS
        if i % N_MIXERS == 0:
            y = moba_attention(h, w_qkv[j], w_o[j], q_gain[j], k_gain[j])
        else:
            y = short_conv(h, w_in[j], conv_w[j], w_out[j])
        x = x + g_m[:, None, :] * y
        h = modulate(x, norm_ffn[i], sh_f, sc_f)
        x = x + g_f[:, None, :] * swiglu(h, w_gate_up[i], w_down[i])
    return x
```

```python
import functools
import math

import jax
import jax.numpy as jnp
from jax import lax
from jax.experimental import pallas as pl
from jax.experimental.pallas import tpu as pltpu

D_MODEL = 1024
N_HEADS = 8
HEAD_DIM = D_MODEL // N_HEADS
BLOCK = 256
TOPK_BLOCKS = 3
CONV_WIDTH = 3
D_FF = int(math.ceil((8 * D_MODEL / 3) / 256) * 256)
EPS = 1e-6

MXU_COLS = 256
SUBLANES = 8
VMEM_LIMIT_BYTES = 56 * 1024 * 1024

NT_DIMS = (((1,), (1,)), ((), ()))


def _const_spec(shape):
    zeros = (0,) * len(shape)
    return pl.BlockSpec(shape, lambda *_: zeros, pipeline_mode=pl.Buffered(1))


def _params(n_grid_axes):
    return pltpu.CompilerParams(
        dimension_semantics=("arbitrary",) * n_grid_axes,
        vmem_limit_bytes=VMEM_LIMIT_BYTES,
    )


def _modulated_norm(x, gain, shift, scale):
    y = x * lax.rsqrt(jnp.mean(x * x, axis=-1, keepdims=True) + EPS)
    return (y * gain) * (1.0 + scale) + shift


def _ff_chunks():
    chunks, start = [], 0
    while start < D_FF:
        size = min(2 * MXU_COLS, D_FF - start)
        chunks.append((start, size))
        start += size
    return chunks


def _swiglu(h_bf16, w_gu_ref, w_down_ref):
    acc = None
    for start, size in _ff_chunks():
        g = jnp.dot(h_bf16, w_gu_ref[:, start:start + size], preferred_element_type=jnp.float32)
        u = jnp.dot(h_bf16, w_gu_ref[:, D_FF + start:D_FF + start + size],
                    preferred_element_type=jnp.float32)
        act = (g * jax.nn.sigmoid(g) * u).astype(jnp.bfloat16)
        part = jnp.dot(act, w_down_ref[start:start + size, :], preferred_element_type=jnp.float32)
        acc = part if acc is None else acc + part
    return acc


def _adaln_kernel(c_ref, w_ref, b_ref, o_ref):
    c = c_ref[...]
    sc = (c * jax.nn.sigmoid(c)).astype(jnp.bfloat16)
    w = w_ref[0].astype(jnp.bfloat16)
    o_ref[0] = jnp.dot(sc, w, preferred_element_type=jnp.float32) + b_ref[0]


def _adaln_mod(c, w_ada, b_ada):
    depth, _, n_out = w_ada.shape
    batch = c.shape[0]
    n_chunk = n_out // 4
    return pl.pallas_call(
        _adaln_kernel,
        out_shape=jax.ShapeDtypeStruct((depth, batch, n_out), jnp.float32),
        grid=(depth, n_out // n_chunk),
        in_specs=[
            pl.BlockSpec((batch, D_MODEL), lambda i, j: (0, 0)),
            pl.BlockSpec((1, D_MODEL, n_chunk), lambda i, j: (i, 0, j)),
            pl.BlockSpec((1, 1, n_chunk), lambda i, j: (i, 0, j)),
        ],
        out_specs=pl.BlockSpec((1, batch, n_chunk), lambda i, j: (i, 0, j)),
        compiler_params=_params(2),
        name="adaln_mod",
    )(c, w_ada, b_ada.reshape(depth, 1, n_out))


def _attn_pre_kernel(x_ref, mod_ref, gain_ref, w_ref, qg_ref, kg_ref,
                     q_ref, k_ref, vt_ref, kmean_ref, *, tm):
    mod = mod_ref[0, 0]
    h = _modulated_norm(x_ref[0], gain_ref[...], mod[0:1], mod[1:2]).astype(jnp.bfloat16)
    n_blk = tm // BLOCK
    heads_per_chunk = MXU_COLS // HEAD_DIM
    for chunk in range(3 * D_MODEL // MXU_COLS):
        r = jnp.dot(h, w_ref[:, chunk * MXU_COLS:(chunk + 1) * MXU_COLS],
                    preferred_element_type=jnp.float32)
        kind = (chunk * heads_per_chunk) // N_HEADS
        for half in range(heads_per_chunk):
            head = (chunk * heads_per_chunk) % N_HEADS + half
            t = r[:, half * HEAD_DIM:(half + 1) * HEAD_DIM]
            if kind == 2:
                for blk in range(n_blk):
                    vt_ref[0, head, blk] = t[blk * BLOCK:(blk + 1) * BLOCK].T.astype(jnp.bfloat16)
                continue
            tn = t * lax.rsqrt(jnp.mean(t * t, axis=-1, keepdims=True) + EPS)
            if kind == 0:
                q_ref[0, head] = ((tn * qg_ref[...]) * (HEAD_DIM ** -0.5)).astype(jnp.bfloat16)
            else:
                kn = tn * kg_ref[...]
                k_ref[0, head] = kn.astype(jnp.bfloat16)
                kmean_ref[0, 0, head] = jnp.mean(kn.reshape(n_blk, BLOCK, HEAD_DIM), axis=1)


def _attn_pre(x, mod, gain, w_qkv_bf16, q_gain, k_gain, *, tm):
    batch, seq, _ = x.shape
    n_tiles = seq // tm
    n_blk = tm // BLOCK
    qk_shape = jax.ShapeDtypeStruct((batch, N_HEADS, seq, HEAD_DIM), jnp.bfloat16)
    qk_spec = pl.BlockSpec((1, N_HEADS, tm, HEAD_DIM), lambda b, t: (b, 0, t, 0))
    return pl.pallas_call(
        functools.partial(_attn_pre_kernel, tm=tm),
        out_shape=(
            qk_shape,
            qk_shape,
            jax.ShapeDtypeStruct((batch, N_HEADS, seq // BLOCK, HEAD_DIM, BLOCK), jnp.bfloat16),
            jax.ShapeDtypeStruct((batch, n_tiles, N_HEADS, n_blk, HEAD_DIM), jnp.float32),
        ),
        grid=(batch, n_tiles),
        in_specs=[
            pl.BlockSpec((1, tm, D_MODEL), lambda b, t: (b, t, 0)),
            pl.BlockSpec((1, 1, 6, D_MODEL), lambda b, t: (0, b, 0, 0)),
            _const_spec((1, D_MODEL)),
            _const_spec((D_MODEL, 3 * D_MODEL)),
            _const_spec((1, HEAD_DIM)),
            _const_spec((1, HEAD_DIM)),
        ],
        out_specs=(
            qk_spec,
            qk_spec,
            pl.BlockSpec((1, N_HEADS, n_blk, HEAD_DIM, BLOCK), lambda b, t: (b, 0, t, 0, 0)),
            pl.BlockSpec((1, 1, N_HEADS, n_blk, HEAD_DIM), lambda b, t: (b, t, 0, 0, 0)),
        ),
        compiler_params=_params(2),
        name="attn_pre",
    )(x, mod, gain, w_qkv_bf16, q_gain, k_gain)


def _moba_kernel(q_ref, k_ref, vt_ref, kmean_ref, o_ref, bias_ref, *, n_blocks):
    kmean = kmean_ref[0, 0]
    km_hi = kmean.astype(jnp.bfloat16)
    km_lo = (kmean - km_hi.astype(jnp.float32)).astype(jnp.bfloat16)
    km_split = jnp.concatenate([km_hi, km_lo], axis=0)
    blk_id = lax.broadcasted_iota(jnp.int32, (n_blocks, BLOCK), 0).astype(jnp.float32)
    key_pos = lax.broadcasted_iota(jnp.int32, (BLOCK, BLOCK), 0)
    qry_pos = lax.broadcasted_iota(jnp.int32, (BLOCK, BLOCK), 1)
    neg_inf = jnp.float32(-jnp.inf)

    def query_block(nb, _):
        q = q_ref[0, 0, nb]

        r2 = lax.dot_general(km_split, q, NT_DIMS, preferred_element_type=jnp.float32)
        route = jnp.where(blk_id < nb.astype(jnp.float32), r2[:n_blocks] + r2[n_blocks:], neg_inf)
        bias = jnp.full((n_blocks, BLOCK), neg_inf, jnp.float32)
        for _ in range(TOPK_BLOCKS):
            best = jnp.max(route, axis=0, keepdims=True)
            first = jnp.min(jnp.where(route == best, blk_id, float(n_blocks)), axis=0, keepdims=True)
            pick = blk_id == first
            bias = jnp.where(pick & (best > neg_inf), 0.0, bias)
            route = jnp.where(pick, neg_inf, route)
        bias_ref[...] = bias

        s = lax.dot_general(k_ref[0, 0, nb], q, NT_DIMS, preferred_element_type=jnp.float32)
        s = jnp.where(key_pos <= qry_pos, s, neg_inf)
        m0 = jnp.max(s, axis=0, keepdims=True)
        p = jnp.exp(s - m0)
        l0 = jnp.sum(p, axis=0, keepdims=True)
        acc0 = jnp.dot(vt_ref[0, 0, nb], p.astype(jnp.bfloat16), preferred_element_type=jnp.float32)

        def key_block(j, carry):
            m, l, acc = carry
            s = lax.dot_general(k_ref[0, 0, j], q, NT_DIMS, preferred_element_type=jnp.float32)
            s = s + bias_ref[pl.ds(j, 1), :]
            m_new = jnp.maximum(m, jnp.max(s, axis=0, keepdims=True))
            alpha = jnp.exp(m - m_new)
            p = jnp.exp(s - m_new)
            l = alpha * l + jnp.sum(p, axis=0, keepdims=True)
            pv = jnp.dot(vt_ref[0, 0, j], p.astype(jnp.bfloat16), preferred_element_type=jnp.float32)
            return m_new, l, alpha * acc + pv

        _, l, acc = lax.fori_loop(0, nb, key_block, (m0, l0, acc0))
        o_ref[0, nb] = (acc / l).T.astype(o_ref.dtype)
        return 0

    lax.fori_loop(0, n_blocks, query_block, 0)


def _moba_attention(q, k, vt, kmean):
    batch, n_heads, seq, _ = q.shape
    n_blocks = seq // BLOCK
    q5 = q.reshape(batch, n_heads, n_blocks, BLOCK, HEAD_DIM)
    k5 = k.reshape(batch, n_heads, n_blocks, BLOCK, HEAD_DIM)
    row_spec = pl.BlockSpec((1, 1, n_blocks, BLOCK, HEAD_DIM), lambda b, h: (b, h, 0, 0, 0))
    out = pl.pallas_call(
        functools.partial(_moba_kernel, n_blocks=n_blocks),
        out_shape=jax.ShapeDtypeStruct((batch, n_blocks, BLOCK, D_MODEL), jnp.bfloat16),
        grid=(batch, n_heads),
        in_specs=[
            row_spec,
            row_spec,
            pl.BlockSpec((1, 1, n_blocks, HEAD_DIM, BLOCK), lambda b, h: (b, h, 0, 0, 0)),
            pl.BlockSpec((1, 1, n_blocks, HEAD_DIM), lambda b, h: (b, h, 0, 0)),
        ],
        out_specs=pl.BlockSpec((1, n_blocks, BLOCK, HEAD_DIM), lambda b, h: (b, 0, 0, h)),
        scratch_shapes=[pltpu.VMEM((n_blocks, BLOCK), jnp.float32)],
        compiler_params=_params(2),
        name="moba_attn",
    )(q5, k5, vt, kmean)
    return out.reshape(batch, seq, D_MODEL)


def _attn_post_kernel(x_ref, o_ref, mod_ref, gain_ref, w_o_ref, w_gu_ref, w_down_ref, out_ref):
    mod = mod_ref[0, 0]
    y = jnp.dot(o_ref[0], w_o_ref[...], preferred_element_type=jnp.float32)
    x = x_ref[0] + mod[2:3] * y
    h = _modulated_norm(x, gain_ref[...], mod[3:4], mod[4:5]).astype(jnp.bfloat16)
    out_ref[0] = x + mod[5:6] * _swiglu(h, w_gu_ref, w_down_ref)


def _attn_post(x, o, mod, gain_ffn, w_o, w_gu, w_down, *, tm):
    batch, seq, _ = x.shape
    tile = lambda b, t: (b, t, 0)
    return pl.pallas_call(
        _attn_post_kernel,
        out_shape=jax.ShapeDtypeStruct(x.shape, x.dtype),
        grid=(batch, seq // tm),
        in_specs=[
            pl.BlockSpec((1, tm, D_MODEL), tile),
            pl.BlockSpec((1, tm, D_MODEL), tile),
            pl.BlockSpec((1, 1, 6, D_MODEL), lambda b, t: (0, b, 0, 0)),
            _const_spec((1, D_MODEL)),
            _const_spec(w_o.shape),
            _const_spec(w_gu.shape),
            _const_spec(w_down.shape),
        ],
        out_specs=pl.BlockSpec((1, tm, D_MODEL), tile),
        compiler_params=_params(2),
        name="attn_post",
    )(x, o, mod, gain_ffn, w_o, w_gu, w_down)


def _conv_layer_kernel(x_ref, halo_ref, mod_ref, gain_mix_ref, gain_ffn_ref, w_in_ref, conv_w_ref,
                       w_out_ref, w_gu_ref, w_down_ref, out_ref, u_ref, *, tm):
    mod = mod_ref[0, 0]
    x = x_ref[0]
    gain_mix = gain_mix_ref[...]
    h = _modulated_norm(x, gain_mix, mod[0:1], mod[1:2]).astype(jnp.bfloat16)
    h_halo = _modulated_norm(halo_ref[0], gain_mix, mod[0:1], mod[1:2]).astype(jnp.bfloat16)
    cu_halo = jnp.dot(h_halo, w_in_ref[:, D_MODEL:], preferred_element_type=jnp.float32)
    u_halo = cu_halo[:, :D_MODEL] * cu_halo[:, D_MODEL:]
    u_ref[0:SUBLANES, :] = jnp.where(pl.program_id(1) > 0, u_halo, 0.0)

    b_gate = jnp.dot(h, w_in_ref[:, :D_MODEL], preferred_element_type=jnp.float32)
    c_gate = jnp.dot(h, w_in_ref[:, D_MODEL:2 * D_MODEL], preferred_element_type=jnp.float32)
    u = c_gate * jnp.dot(h, w_in_ref[:, 2 * D_MODEL:], preferred_element_type=jnp.float32)
    u_ref[SUBLANES:, :] = u
    y = conv_w_ref[CONV_WIDTH - 1:CONV_WIDTH, :] * u
    for tap in range(CONV_WIDTH - 1):
        back = CONV_WIDTH - 1 - tap
        y = y + conv_w_ref[tap:tap + 1, :] * u_ref[SUBLANES - back:SUBLANES - back + tm, :]
    mix = jnp.dot((b_gate * y).astype(jnp.bfloat16), w_out_ref[...], preferred_element_type=jnp.float32)
    x = x + mod[2:3] * mix
    h = _modulated_norm(x, gain_ffn_ref[...], mod[3:4], mod[4:5]).astype(jnp.bfloat16)
    out_ref[0] = x + mod[5:6] * _swiglu(h, w_gu_ref, w_down_ref)


def _conv_layer(x, mod, gain_mix, gain_ffn, w_in, conv_w, w_out, w_gu, w_down, *, tm):
    batch, seq, _ = x.shape
    tile = lambda b, t: (b, t, 0)
    halo_blocks_per_tile = tm // SUBLANES
    return pl.pallas_call(
        functools.partial(_conv_layer_kernel, tm=tm),
        out_shape=jax.ShapeDtypeStruct(x.shape, x.dtype),
        grid=(batch, seq // tm),
        in_specs=[
            pl.BlockSpec((1, tm, D_MODEL), tile),
            pl.BlockSpec((1, SUBLANES, D_MODEL),
                         lambda b, t: (b, jnp.maximum(t * halo_blocks_per_tile - 1, 0), 0)),
            pl.BlockSpec((1, 1, 6, D_MODEL), lambda b, t: (1, b, 0, 0)),
            _const_spec((1, D_MODEL)),
            _const_spec((1, D_MODEL)),
            _const_spec(w_in.shape),
            _const_spec(conv_w.shape),
            _const_spec(w_out.shape),
            _const_spec(w_gu.shape),
            _const_spec(w_down.shape),
        ],
        out_specs=pl.BlockSpec((1, tm, D_MODEL), tile),
        scratch_shapes=[pltpu.VMEM((tm + SUBLANES, D_MODEL), jnp.float32)],
        compiler_params=_params(2),
        name="conv_layer",
    )(x, x, mod, gain_mix, gain_ffn, w_in, conv_w, w_out, w_gu, w_down)


def kernel(x, c, w_ada, b_ada, norm_mix, norm_ffn, w_qkv, w_o, q_gain, k_gain,
           w_in, conv_w, w_out, w_gate_up, w_down):
    batch, seq, _ = x.shape
    depth = w_ada.shape[0]
    assert depth == 2 and seq % BLOCK == 0
    bf16 = jnp.bfloat16
    mod = _adaln_mod(c, w_ada, b_ada).reshape(depth, batch, 6, D_MODEL)

    tm_pre, tm_ffn = 512, 512
    q, k, vt, kmean = _attn_pre(x, mod, norm_mix[0:1], w_qkv[0].astype(bf16), q_gain[0:1], k_gain[0:1],
                                tm=tm_pre)
    kmean = kmean.transpose(0, 2, 1, 3, 4).reshape(batch, N_HEADS, seq // BLOCK, HEAD_DIM)
    o = _moba_attention(q, k, vt, kmean)
    x = _attn_post(x, o, mod, norm_ffn[0:1], w_o[0].astype(bf16), w_gate_up[0].astype(bf16),
                   w_down[0].astype(bf16), tm=tm_ffn)
    x = _conv_layer(x, mod, norm_mix[1:2], norm_ffn[1:2], w_in[0].astype(bf16), conv_w[0],
                    w_out[0].astype(bf16), w_gate_up[1].astype(bf16), w_down[1].astype(bf16), tm=tm_ffn)
    return x
```

```python
import functools
import math

import jax
import jax.numpy as jnp
from jax import lax
from jax.experimental import pallas as pl
from jax.experimental.pallas import tpu as pltpu

D_MODEL = 1024
N_HEADS = 8
HEAD_DIM = D_MODEL // N_HEADS
BLOCK = 256
TOPK_BLOCKS = 3
CONV_WIDTH = 3
D_FF = int(math.ceil((8 * D_MODEL / 3) / 256) * 256)
EPS = 1e-6

MXU_COLS = 256
SUBLANES = 8
DENOM_ROWS = 16
VMEM_LIMIT_BYTES = 56 * 1024 * 1024

MASKED = -1e30
Q_SCALE = HEAD_DIM ** -0.5 * math.log2(math.e)


def _const_spec(shape):
    zeros = (0,) * len(shape)
    return pl.BlockSpec(shape, lambda *_: zeros, pipeline_mode=pl.Buffered(1))


def _params(n_grid_axes, flags=None):
    return pltpu.CompilerParams(
        dimension_semantics=("arbitrary",) * n_grid_axes,
        vmem_limit_bytes=VMEM_LIMIT_BYTES,
        flags=flags,
    )


def _modulated_norm(x, gain, shift, scale):
    y = x * lax.rsqrt(jnp.mean(x * x, axis=-1, keepdims=True) + EPS)
    return (y * gain) * (1.0 + scale) + shift


def _ff_chunks():
    chunks, start = [], 0
    while start < D_FF:
        size = min(2 * MXU_COLS, D_FF - start)
        chunks.append((start, size))
        start += size
    return chunks


def _swiglu(h_bf16, w_gu_ref, w_down_ref):
    acc = None
    for start, size in _ff_chunks():
        g = jnp.dot(h_bf16, w_gu_ref[:, start:start + size], preferred_element_type=jnp.float32)
        u = jnp.dot(h_bf16, w_gu_ref[:, D_FF + start:D_FF + start + size],
                    preferred_element_type=jnp.float32)
        act = (g * jax.nn.sigmoid(g) * u).astype(jnp.bfloat16)
        part = jnp.dot(act, w_down_ref[start:start + size, :], preferred_element_type=jnp.float32)
        acc = part if acc is None else acc + part
    return acc


def _adaln_kernel(c_ref, w_ref, b_ref, o_ref):
    c = c_ref[...]
    sc = (c * jax.nn.sigmoid(c)).astype(jnp.bfloat16)
    w = w_ref[0].astype(jnp.bfloat16)
    o_ref[0] = jnp.dot(sc, w, preferred_element_type=jnp.float32) + b_ref[0]


def _adaln_mod(c, w_ada, b_ada):
    depth, _, n_out = w_ada.shape
    batch = c.shape[0]
    n_chunk = n_out // 4
    return pl.pallas_call(
        _adaln_kernel,
        out_shape=jax.ShapeDtypeStruct((depth, batch, n_out), jnp.float32),
        grid=(depth, n_out // n_chunk),
        in_specs=[
            pl.BlockSpec((batch, D_MODEL), lambda i, j: (0, 0)),
            pl.BlockSpec((1, D_MODEL, n_chunk), lambda i, j: (i, 0, j)),
            pl.BlockSpec((1, 1, n_chunk), lambda i, j: (i, 0, j)),
        ],
        out_specs=pl.BlockSpec((1, batch, n_chunk), lambda i, j: (i, 0, j)),
        compiler_params=_params(2),
        name="adaln_mod",
    )(c, w_ada, b_ada.reshape(depth, 1, n_out))


def _attn_pre_kernel(x_ref, mod_ref, gain_ref, w_ref, qg_ref, kg_ref,
                     qt_ref, k_ref, vt_ref, kmean_ref, *, tm):
    mod = mod_ref[0, 0]
    h = _modulated_norm(x_ref[0], gain_ref[...], mod[0:1], mod[1:2]).astype(jnp.bfloat16)
    n_blk = tm // BLOCK
    heads_per_chunk = MXU_COLS // HEAD_DIM
    for chunk in range(3 * D_MODEL // MXU_COLS):
        r = jnp.dot(h, w_ref[:, chunk * MXU_COLS:(chunk + 1) * MXU_COLS],
                    preferred_element_type=jnp.float32)
        kind = (chunk * heads_per_chunk) // N_HEADS
        for half in range(heads_per_chunk):
            head = (chunk * heads_per_chunk) % N_HEADS + half
            t = r[:, half * HEAD_DIM:(half + 1) * HEAD_DIM]
            if kind == 2:
                for blk in range(n_blk):
                    vt_ref[0, head, blk] = t[blk * BLOCK:(blk + 1) * BLOCK].T.astype(jnp.bfloat16)
                continue
            tn = t * lax.rsqrt(jnp.mean(t * t, axis=-1, keepdims=True) + EPS)
            if kind == 0:
                qn = (tn * qg_ref[...]) * Q_SCALE
                for blk in range(n_blk):
                    qt_ref[0, head, blk] = qn[blk * BLOCK:(blk + 1) * BLOCK].T.astype(jnp.bfloat16)
            else:
                kn = tn * kg_ref[...]
                k_ref[0, head] = kn.astype(jnp.bfloat16)
                kmean_ref[0, 0, head] = jnp.mean(kn.reshape(n_blk, BLOCK, HEAD_DIM), axis=1)


def _attn_pre(x, mod, gain, w_qkv_bf16, q_gain, k_gain, *, tm):
    batch, seq, _ = x.shape
    n_tiles = seq // tm
    n_blk = tm // BLOCK
    t_shape = jax.ShapeDtypeStruct((batch, N_HEADS, seq // BLOCK, HEAD_DIM, BLOCK), jnp.bfloat16)
    t_spec = pl.BlockSpec((1, N_HEADS, n_blk, HEAD_DIM, BLOCK), lambda b, t: (b, 0, t, 0, 0))
    return pl.pallas_call(
        functools.partial(_attn_pre_kernel, tm=tm),
        out_shape=(
            t_shape,
            jax.ShapeDtypeStruct((batch, N_HEADS, seq, HEAD_DIM), jnp.bfloat16),
            t_shape,
            jax.ShapeDtypeStruct((batch, n_tiles, N_HEADS, n_blk, HEAD_DIM), jnp.float32),
        ),
        grid=(batch, n_tiles),
        in_specs=[
            pl.BlockSpec((1, tm, D_MODEL), lambda b, t: (b, t, 0)),
            pl.BlockSpec((1, 1, 6, D_MODEL), lambda b, t: (0, b, 0, 0)),
            _const_spec((1, D_MODEL)),
            _const_spec((D_MODEL, 3 * D_MODEL)),
            _const_spec((1, HEAD_DIM)),
            _const_spec((1, HEAD_DIM)),
        ],
        out_specs=(
            t_spec,
            pl.BlockSpec((1, N_HEADS, tm, HEAD_DIM), lambda b, t: (b, 0, t, 0)),
            t_spec,
            pl.BlockSpec((1, 1, N_HEADS, n_blk, HEAD_DIM), lambda b, t: (b, t, 0, 0, 0)),
        ),
        compiler_params=_params(2),
        name="attn_pre",
    )(x, mod, gain, w_qkv_bf16, q_gain, k_gain)


def _moba_kernel(qt_ref, k_ref, vt_ref, kmean_ref, sel_ref, o_ref,
                 q2t_ref, s_ref, mb_ref, m_ref, acc_ref, *, n_blocks):
    nb = pl.program_id(1)
    blk_id = lax.broadcasted_iota(jnp.int32, (n_blocks, BLOCK), 0).astype(jnp.float32)
    key_pos = lax.broadcasted_iota(jnp.int32, (BLOCK, BLOCK), 0)
    qry_pos = lax.broadcasted_iota(jnp.int32, (BLOCK, BLOCK), 1)
    neg_inf = jnp.float32(-jnp.inf)
    pad_rows = BLOCK - HEAD_DIM - n_blocks

    for h in range(N_HEADS):
        qt = qt_ref[0, h, 0]
        kmean = kmean_ref[0, h]
        km_hi = kmean.astype(jnp.bfloat16)
        km_lo = (kmean - km_hi.astype(jnp.float32)).astype(jnp.bfloat16)
        r2 = jnp.dot(jnp.concatenate([km_hi, km_lo], axis=0), qt, preferred_element_type=jnp.float32)
        route = jnp.where(blk_id < nb.astype(jnp.float32), r2[:n_blocks] + r2[n_blocks:], neg_inf)
        bias = jnp.full((n_blocks, BLOCK), MASKED, jnp.float32)
        for _ in range(TOPK_BLOCKS):
            best = jnp.max(route, axis=0, keepdims=True)
            first = jnp.min(jnp.where(route == best, blk_id, float(n_blocks)), axis=0, keepdims=True)
            pick = blk_id == first
            bias = jnp.where(pick & (best > neg_inf), 0.0, bias)
            route = jnp.where(pick, neg_inf, route)
        q2t_ref[h, 0:HEAD_DIM, :] = qt
        q2t_ref[h, HEAD_DIM:HEAD_DIM + n_blocks, :] = bias.astype(jnp.bfloat16)
        q2t_ref[h, HEAD_DIM + n_blocks:, :] = jnp.zeros((pad_rows, BLOCK), jnp.bfloat16)

    ones_rows = jnp.ones((DENOM_ROWS, BLOCK), jnp.bfloat16)

    def score_own(h, slot):
        s = jnp.dot(k_ref[0, h, nb], qt_ref[0, h, 0], preferred_element_type=jnp.float32)
        s = jnp.where(key_pos <= qry_pos, s, MASKED)
        s_ref[slot, h] = s
        mb_ref[slot, h] = jnp.max(s, axis=0, keepdims=True)

    def score_past(h, j, slot):
        keys = jnp.concatenate([k_ref[0, h, j], sel_ref[j]], axis=1)
        s = jnp.dot(keys, q2t_ref[h], preferred_element_type=jnp.float32)
        s_ref[slot, h] = s
        mb_ref[slot, h] = jnp.max(s, axis=0, keepdims=True)

    def accumulate(h, j, slot, first):
        m_blk = mb_ref[slot, h]
        if first:
            m_new = m_blk
        else:
            m = m_ref[h]
            m_new = jnp.maximum(m, m_blk)
            alpha = jnp.exp2(m - m_new)
        p = jnp.exp2(s_ref[slot, h] - m_new).astype(jnp.bfloat16)
        pv = jnp.dot(jnp.concatenate([vt_ref[0, h, j], ones_rows], axis=0), p,
                     preferred_element_type=jnp.float32)
        m_ref[h] = m_new
        acc_ref[h] = pv if first else alpha * acc_ref[h] + pv

    def step(j, slot, j_next):
        for h in range(N_HEADS):
            score_past(h, j_next, 1 - slot)
            accumulate(h, j, slot, False)

    for h in range(N_HEADS):
        score_own(h, 1)
    for h in range(N_HEADS):
        score_past(h, 0, 0)
        accumulate(h, nb, 1, True)

    def key_block_pair(i, _):
        j = 2 * i
        step(j, 0, j + 1)
        step(j + 1, 1, jnp.minimum(j + 2, n_blocks - 1))
        return 0

    lax.fori_loop(0, lax.shift_right_logical(nb + 1, 1), key_block_pair, 0)
    for h in range(N_HEADS):
        out_t = acc_ref[h, 0:HEAD_DIM, :] / acc_ref[h, HEAD_DIM:HEAD_DIM + 1, :]
        o_ref[0, :, h * HEAD_DIM:(h + 1) * HEAD_DIM] = out_t.T.astype(o_ref.dtype)


def _moba_attention(qt, k, vt, kmean):
    batch, n_heads, n_blocks, _, _ = qt.shape
    seq = n_blocks * BLOCK
    k5 = k.reshape(batch, n_heads, n_blocks, BLOCK, HEAD_DIM)
    sel = jnp.broadcast_to(jnp.eye(n_blocks, 128, dtype=jnp.bfloat16)[:, None, :], (n_blocks, BLOCK, 128))
    per_batch = lambda b, nb: (b,) + (0,) * 4
    return pl.pallas_call(
        functools.partial(_moba_kernel, n_blocks=n_blocks),
        out_shape=jax.ShapeDtypeStruct((batch, seq, D_MODEL), jnp.bfloat16),
        grid=(batch, n_blocks),
        in_specs=[
            pl.BlockSpec((1, n_heads, 1, HEAD_DIM, BLOCK), lambda b, nb: (b, 0, nb, 0, 0)),
            pl.BlockSpec((1, n_heads, n_blocks, BLOCK, HEAD_DIM), per_batch, pipeline_mode=pl.Buffered(1)),
            pl.BlockSpec((1, n_heads, n_blocks, HEAD_DIM, BLOCK), per_batch, pipeline_mode=pl.Buffered(1)),
            pl.BlockSpec((1, n_heads, n_blocks, HEAD_DIM), lambda b, nb: (b, 0, 0, 0)),
            _const_spec(sel.shape),
        ],
        out_specs=pl.BlockSpec((1, BLOCK, D_MODEL), lambda b, nb: (b, nb, 0)),
        scratch_shapes=[
            pltpu.VMEM((n_heads, BLOCK, BLOCK), jnp.bfloat16),
            pltpu.VMEM((2, n_heads, BLOCK, BLOCK), jnp.float32),
            pltpu.VMEM((2, n_heads, 1, BLOCK), jnp.float32),
            pltpu.VMEM((n_heads, 1, BLOCK), jnp.float32),
            pltpu.VMEM((n_heads, HEAD_DIM + DENOM_ROWS, BLOCK), jnp.float32),
        ],
        compiler_params=_params(2),
        name="moba_attn",
    )(qt, k5, vt, kmean, sel)


def _attn_post_kernel(x_ref, o_ref, mod_ref, gain_ref, w_o_ref, w_gu_ref, w_down_ref, out_ref):
    mod = mod_ref[0, 0]
    y = jnp.dot(o_ref[0], w_o_ref[...], preferred_element_type=jnp.float32)
    x = x_ref[0] + mod[2:3] * y
    h = _modulated_norm(x, gain_ref[...], mod[3:4], mod[4:5]).astype(jnp.bfloat16)
    out_ref[0] = x + mod[5:6] * _swiglu(h, w_gu_ref, w_down_ref)


def _attn_post(x, o, mod, gain_ffn, w_o, w_gu, w_down, *, tm):
    batch, seq, _ = x.shape
    tile = lambda b, t: (b, t, 0)
    return pl.pallas_call(
        _attn_post_kernel,
        out_shape=jax.ShapeDtypeStruct(x.shape, x.dtype),
        grid=(batch, seq // tm),
        in_specs=[
            pl.BlockSpec((1, tm, D_MODEL), tile),
            pl.BlockSpec((1, tm, D_MODEL), tile),
            pl.BlockSpec((1, 1, 6, D_MODEL), lambda b, t: (0, b, 0, 0)),
            _const_spec((1, D_MODEL)),
            _const_spec(w_o.shape),
            _const_spec(w_gu.shape),
            _const_spec(w_down.shape),
        ],
        out_specs=pl.BlockSpec((1, tm, D_MODEL), tile),
        compiler_params=_params(2),
        name="attn_post",
    )(x, o, mod, gain_ffn, w_o, w_gu, w_down)


def _conv_layer_kernel(x_ref, halo_ref, mod_ref, gain_mix_ref, gain_ffn_ref, w_in_ref, conv_w_ref,
                       w_out_ref, w_gu_ref, w_down_ref, out_ref, u_ref, *, tm):
    mod = mod_ref[0, 0]
    x = x_ref[0]
    gain_mix = gain_mix_ref[...]
    h = _modulated_norm(x, gain_mix, mod[0:1], mod[1:2]).astype(jnp.bfloat16)
    h_halo = _modulated_norm(halo_ref[0], gain_mix, mod[0:1], mod[1:2]).astype(jnp.bfloat16)
    cu_halo = jnp.dot(h_halo, w_in_ref[:, D_MODEL:], preferred_element_type=jnp.float32)
    u_halo = cu_halo[:, :D_MODEL] * cu_halo[:, D_MODEL:]
    u_ref[0:SUBLANES, :] = jnp.where(pl.program_id(1) > 0, u_halo, 0.0)

    b_gate = jnp.dot(h, w_in_ref[:, :D_MODEL], preferred_element_type=jnp.float32)
    c_gate = jnp.dot(h, w_in_ref[:, D_MODEL:2 * D_MODEL], preferred_element_type=jnp.float32)
    u = c_gate * jnp.dot(h, w_in_ref[:, 2 * D_MODEL:], preferred_element_type=jnp.float32)
    u_ref[SUBLANES:, :] = u
    y = conv_w_ref[CONV_WIDTH - 1:CONV_WIDTH, :] * u
    for tap in range(CONV_WIDTH - 1):
        back = CONV_WIDTH - 1 - tap
        y = y + conv_w_ref[tap:tap + 1, :] * u_ref[SUBLANES - back:SUBLANES - back + tm, :]
    mix = jnp.dot((b_gate * y).astype(jnp.bfloat16), w_out_ref[...], preferred_element_type=jnp.float32)
    x = x + mod[2:3] * mix
    h = _modulated_norm(x, gain_ffn_ref[...], mod[3:4], mod[4:5]).astype(jnp.bfloat16)
    out_ref[0] = x + mod[5:6] * _swiglu(h, w_gu_ref, w_down_ref)


def _conv_layer(x, mod, gain_mix, gain_ffn, w_in, conv_w, w_out, w_gu, w_down, *, tm):
    batch, seq, _ = x.shape
    tile = lambda b, t: (b, t, 0)
    halo_blocks_per_tile = tm // SUBLANES
    return pl.pallas_call(
        functools.partial(_conv_layer_kernel, tm=tm),
        out_shape=jax.ShapeDtypeStruct(x.shape, x.dtype),
        grid=(batch, seq // tm),
        in_specs=[
            pl.BlockSpec((1, tm, D_MODEL), tile),
            pl.BlockSpec((1, SUBLANES, D_MODEL),
                         lambda b, t: (b, jnp.maximum(t * halo_blocks_per_tile - 1, 0), 0)),
            pl.BlockSpec((1, 1, 6, D_MODEL), lambda b, t: (1, b, 0, 0)),
            _const_spec((1, D_MODEL)),
            _const_spec((1, D_MODEL)),
            _const_spec(w_in.shape),
            _const_spec(conv_w.shape),
            _const_spec(w_out.shape),
            _const_spec(w_gu.shape),
            _const_spec(w_down.shape),
        ],
        out_specs=pl.BlockSpec((1, tm, D_MODEL), tile),
        scratch_shapes=[pltpu.VMEM((tm + SUBLANES, D_MODEL), jnp.float32)],
        compiler_params=_params(2),
        name="conv_layer",
    )(x, x, mod, gain_mix, gain_ffn, w_in, conv_w, w_out, w_gu, w_down)


def kernel(x, c, w_ada, b_ada, norm_mix, norm_ffn, w_qkv, w_o, q_gain, k_gain,
           w_in, conv_w, w_out, w_gate_up, w_down):
    batch, seq, _ = x.shape
    depth = w_ada.shape[0]
    assert depth == 2 and seq % BLOCK == 0
    bf16 = jnp.bfloat16
    mod = _adaln_mod(c, w_ada, b_ada).reshape(depth, batch, 6, D_MODEL)

    tm_pre, tm_ffn = 512, 512
    q, k, vt, kmean = _attn_pre(x, mod, norm_mix[0:1], w_qkv[0].astype(bf16), q_gain[0:1], k_gain[0:1],
                                tm=tm_pre)
    kmean = kmean.transpose(0, 2, 1, 3, 4).reshape(batch, N_HEADS, seq // BLOCK, HEAD_DIM)
    o = _moba_attention(q, k, vt, kmean)
    x = _attn_post(x, o, mod, norm_ffn[0:1], w_o[0].astype(bf16), w_gate_up[0].astype(bf16),
                   w_down[0].astype(bf16), tm=tm_ffn)
    x = _conv_layer(x, mod, norm_mix[1:2], norm_ffn[1:2], w_in[0].astype(bf16), conv_w[0],
                    w_out[0].astype(bf16), w_gate_up[1].astype(bf16), w_down[1].astype(bf16), tm=tm_ffn)
    return x
```

```python
import functools
import math

import jax
import jax.numpy as jnp
from jax import lax
from jax.experimental import pallas as pl
from jax.experimental.pallas import tpu as pltpu

D_MODEL = 1024
N_HEADS = 8
HEAD_DIM = D_MODEL // N_HEADS
BLOCK = 256
TOPK_BLOCKS = 3
CONV_WIDTH = 3
D_FF = int(math.ceil((8 * D_MODEL / 3) / 256) * 256)
EPS = 1e-6

MXU_COLS = 256
LANES = 128
SUBLANES = 8
DENOM_ROWS = 16
VMEM_LIMIT_BYTES = 56 * 1024 * 1024

MASKED = -1e30
Q_SCALE = HEAD_DIM ** -0.5 * math.log2(math.e)


def _const_spec(shape):
    zeros = (0,) * len(shape)
    return pl.BlockSpec(shape, lambda *_: zeros, pipeline_mode=pl.Buffered(1))


def _params(n_grid_axes, flags=None):
    return pltpu.CompilerParams(
        dimension_semantics=("arbitrary",) * n_grid_axes,
        vmem_limit_bytes=VMEM_LIMIT_BYTES,
        flags=flags,
    )


def _modulated_norm(x, gain, shift, scale):
    y = x * lax.rsqrt(jnp.mean(x * x, axis=-1, keepdims=True) + EPS)
    return (y * gain) * (1.0 + scale) + shift


def _ff_chunks():
    chunks, start = [], 0
    while start < D_FF:
        size = min(2 * MXU_COLS, D_FF - start)
        chunks.append((start, size))
        start += size
    return chunks


def _swiglu(h_bf16, w_gu_ref, w_down_ref):
    acc = None
    for start, size in _ff_chunks():
        g = jnp.dot(h_bf16, w_gu_ref[:, start:start + size], preferred_element_type=jnp.float32)
        u = jnp.dot(h_bf16, w_gu_ref[:, D_FF + start:D_FF + start + size],
                    preferred_element_type=jnp.float32)
        act = (g * jax.nn.sigmoid(g) * u).astype(jnp.bfloat16)
        part = jnp.dot(act, w_down_ref[start:start + size, :], preferred_element_type=jnp.float32)
        acc = part if acc is None else acc + part
    return acc


def _adaln_kernel(c_ref, w_ref, b_ref, o_ref):
    c = c_ref[...]
    sc = (c * jax.nn.sigmoid(c)).astype(jnp.bfloat16)
    w = w_ref[0].astype(jnp.bfloat16)
    o_ref[0] = jnp.dot(sc, w, preferred_element_type=jnp.float32) + b_ref[0]


def _adaln_mod(c, w_ada, b_ada):
    depth, _, n_out = w_ada.shape
    batch = c.shape[0]
    n_chunk = n_out // 4
    return pl.pallas_call(
        _adaln_kernel,
        out_shape=jax.ShapeDtypeStruct((depth, batch, n_out), jnp.float32),
        grid=(depth, n_out // n_chunk),
        in_specs=[
            pl.BlockSpec((batch, D_MODEL), lambda i, j: (0, 0)),
            pl.BlockSpec((1, D_MODEL, n_chunk), lambda i, j: (i, 0, j)),
            pl.BlockSpec((1, 1, n_chunk), lambda i, j: (i, 0, j)),
        ],
        out_specs=pl.BlockSpec((1, batch, n_chunk), lambda i, j: (i, 0, j)),
        compiler_params=_params(2),
        name="adaln_mod",
    )(c, w_ada, b_ada.reshape(depth, 1, n_out))


def _attn_pre_kernel(x_ref, mod_ref, gain_ref, w_ref, qg_ref, kg_ref,
                     qt_ref, k_ref, vt_ref, kmean_ref, *, tm):
    mod = mod_ref[0, 0]
    h = _modulated_norm(x_ref[0], gain_ref[...], mod[0:1], mod[1:2]).astype(jnp.bfloat16)
    n_blk = tm // BLOCK
    heads_per_chunk = MXU_COLS // HEAD_DIM
    for chunk in range(3 * D_MODEL // MXU_COLS):
        r = jnp.dot(h, w_ref[:, chunk * MXU_COLS:(chunk + 1) * MXU_COLS],
                    preferred_element_type=jnp.float32)
        kind = (chunk * heads_per_chunk) // N_HEADS
        for half in range(heads_per_chunk):
            head = (chunk * heads_per_chunk) % N_HEADS + half
            t = r[:, half * HEAD_DIM:(half + 1) * HEAD_DIM]
            if kind == 2:
                for blk in range(n_blk):
                    vt_ref[0, head, blk] = t[blk * BLOCK:(blk + 1) * BLOCK].T.astype(jnp.bfloat16)
                continue
            tn = t * lax.rsqrt(jnp.mean(t * t, axis=-1, keepdims=True) + EPS)
            if kind == 0:
                qn = (tn * qg_ref[...]) * Q_SCALE
                for blk in range(n_blk):
                    qt_ref[0, head, blk] = qn[blk * BLOCK:(blk + 1) * BLOCK].T.astype(jnp.bfloat16)
            else:
                kn = tn * kg_ref[...]
                k_ref[0, head] = kn.astype(jnp.bfloat16)
                kmean_ref[0, 0, head] = jnp.mean(kn.reshape(n_blk, BLOCK, HEAD_DIM), axis=1)


def _attn_pre(x, mod, gain, w_qkv_bf16, q_gain, k_gain, *, tm):
    batch, seq, _ = x.shape
    n_tiles = seq // tm
    n_blk = tm // BLOCK
    t_shape = jax.ShapeDtypeStruct((batch, N_HEADS, seq // BLOCK, HEAD_DIM, BLOCK), jnp.bfloat16)
    t_spec = pl.BlockSpec((1, N_HEADS, n_blk, HEAD_DIM, BLOCK), lambda b, t: (b, 0, t, 0, 0))
    return pl.pallas_call(
        functools.partial(_attn_pre_kernel, tm=tm),
        out_shape=(
            t_shape,
            jax.ShapeDtypeStruct((batch, N_HEADS, seq, HEAD_DIM), jnp.bfloat16),
            t_shape,
            jax.ShapeDtypeStruct((batch, n_tiles, N_HEADS, n_blk, HEAD_DIM), jnp.float32),
        ),
        grid=(batch, n_tiles),
        in_specs=[
            pl.BlockSpec((1, tm, D_MODEL), lambda b, t: (b, t, 0)),
            pl.BlockSpec((1, 1, 6, D_MODEL), lambda b, t: (0, b, 0, 0)),
            _const_spec((1, D_MODEL)),
            _const_spec((D_MODEL, 3 * D_MODEL)),
            _const_spec((1, HEAD_DIM)),
            _const_spec((1, HEAD_DIM)),
        ],
        out_specs=(
            t_spec,
            pl.BlockSpec((1, N_HEADS, tm, HEAD_DIM), lambda b, t: (b, 0, t, 0)),
            t_spec,
            pl.BlockSpec((1, 1, N_HEADS, n_blk, HEAD_DIM), lambda b, t: (b, t, 0, 0, 0)),
        ),
        compiler_params=_params(2),
        name="attn_pre",
    )(x, mod, gain, w_qkv_bf16, q_gain, k_gain)


def _moba_kernel(qt_ref, k_ref, vt_ref, kmean_ref, o_ref,
                 q2t_ref, s_ref, mb_ref, m_ref, acc_ref, *, n_blocks):
    nb = pl.program_id(1)
    blk_id = lax.broadcasted_iota(jnp.int32, (n_blocks, BLOCK), 0).astype(jnp.float32)
    key_pos = lax.broadcasted_iota(jnp.int32, (BLOCK, BLOCK), 0)
    qry_pos = lax.broadcasted_iota(jnp.int32, (BLOCK, BLOCK), 1)
    lane_id = lax.broadcasted_iota(jnp.int32, (BLOCK, LANES), 1)
    neg_inf = jnp.float32(-jnp.inf)
    pad_rows = BLOCK - HEAD_DIM - n_blocks

    for h in range(N_HEADS):
        qt = qt_ref[0, h, 0]
        kmean = kmean_ref[0, h]
        km_hi = kmean.astype(jnp.bfloat16)
        km_lo = (kmean - km_hi.astype(jnp.float32)).astype(jnp.bfloat16)
        r2 = jnp.dot(jnp.concatenate([km_hi, km_lo], axis=0), qt, preferred_element_type=jnp.float32)
        route = jnp.where(blk_id < nb.astype(jnp.float32), r2[:n_blocks] + r2[n_blocks:], neg_inf)
        bias = jnp.full((n_blocks, BLOCK), MASKED, jnp.float32)
        for _ in range(TOPK_BLOCKS):
            best = jnp.max(route, axis=0, keepdims=True)
            first = jnp.min(jnp.where(route == best, blk_id, float(n_blocks)), axis=0, keepdims=True)
            pick = blk_id == first
            bias = jnp.where(pick & (best > neg_inf), 0.0, bias)
            route = jnp.where(pick, neg_inf, route)
        q2t_ref[h, 0:HEAD_DIM, :] = qt
        q2t_ref[h, HEAD_DIM:HEAD_DIM + n_blocks, :] = bias.astype(jnp.bfloat16)
        q2t_ref[h, HEAD_DIM + n_blocks:, :] = jnp.zeros((pad_rows, BLOCK), jnp.bfloat16)

    ones_rows = jnp.ones((DENOM_ROWS, BLOCK), jnp.bfloat16)

    def score_own(h, slot):
        s = jnp.dot(k_ref[0, h, nb], qt_ref[0, h, 0], preferred_element_type=jnp.float32)
        s = jnp.where(key_pos <= qry_pos, s, MASKED)
        s_ref[slot, h] = s
        mb_ref[slot, h] = jnp.max(s, axis=0, keepdims=True)

    def block_one_hot(j):
        return jnp.where(lane_id == j, 1.0, 0.0).astype(jnp.bfloat16)

    def score_past(h, j, one_hot, slot):
        keys = jnp.concatenate([k_ref[0, h, j], one_hot], axis=1)
        s = jnp.dot(keys, q2t_ref[h], preferred_element_type=jnp.float32)
        s_ref[slot, h] = s
        mb_ref[slot, h] = jnp.max(s, axis=0, keepdims=True)

    def accumulate(h, j, slot, first):
        m_blk = mb_ref[slot, h]
        if first:
            m_new = m_blk
        else:
            m = m_ref[h]
            m_new = jnp.maximum(m, m_blk)
            alpha = jnp.exp2(m - m_new)
        p = jnp.exp2(s_ref[slot, h] - m_new).astype(jnp.bfloat16)
        pv = jnp.dot(jnp.concatenate([vt_ref[0, h, j], ones_rows], axis=0), p,
                     preferred_element_type=jnp.float32)
        m_ref[h] = m_new
        acc_ref[h] = pv if first else alpha * acc_ref[h] + pv

    def step(j, slot, j_next):
        one_hot = block_one_hot(j_next)
        for h in range(N_HEADS):
            score_past(h, j_next, one_hot, 1 - slot)
            accumulate(h, j, slot, False)

    for h in range(N_HEADS):
        score_own(h, 1)
    one_hot = block_one_hot(0)
    for h in range(N_HEADS):
        score_past(h, 0, one_hot, 0)
        accumulate(h, nb, 1, True)

    def key_block_pair(i, _):
        j = 2 * i
        step(j, 0, j + 1)
        step(j + 1, 1, jnp.minimum(j + 2, n_blocks - 1))
        return 0

    lax.fori_loop(0, lax.shift_right_logical(nb + 1, 1), key_block_pair, 0)
    for h in range(N_HEADS):
        out_t = acc_ref[h, 0:HEAD_DIM, :] / acc_ref[h, HEAD_DIM:HEAD_DIM + 1, :]
        o_ref[0, :, h * HEAD_DIM:(h + 1) * HEAD_DIM] = out_t.T.astype(o_ref.dtype)


def _moba_attention(qt, k, vt, kmean):
    batch, n_heads, n_blocks, _, _ = qt.shape
    seq = n_blocks * BLOCK
    k5 = k.reshape(batch, n_heads, n_blocks, BLOCK, HEAD_DIM)
    per_batch = lambda b, nb: (b,) + (0,) * 4
    return pl.pallas_call(
        functools.partial(_moba_kernel, n_blocks=n_blocks),
        out_shape=jax.ShapeDtypeStruct((batch, seq, D_MODEL), jnp.bfloat16),
        grid=(batch, n_blocks),
        in_specs=[
            pl.BlockSpec((1, n_heads, 1, HEAD_DIM, BLOCK), lambda b, nb: (b, 0, nb, 0, 0)),
            pl.BlockSpec((1, n_heads, n_blocks, BLOCK, HEAD_DIM), per_batch, pipeline_mode=pl.Buffered(1)),
            pl.BlockSpec((1, n_heads, n_blocks, HEAD_DIM, BLOCK), per_batch, pipeline_mode=pl.Buffered(1)),
            pl.BlockSpec((1, n_heads, n_blocks, HEAD_DIM), lambda b, nb: (b, 0, 0, 0)),
        ],
        out_specs=pl.BlockSpec((1, BLOCK, D_MODEL), lambda b, nb: (b, nb, 0)),
        scratch_shapes=[
            pltpu.VMEM((n_heads, BLOCK, BLOCK), jnp.bfloat16),
            pltpu.VMEM((2, n_heads, BLOCK, BLOCK), jnp.float32),
            pltpu.VMEM((2, n_heads, 1, BLOCK), jnp.float32),
            pltpu.VMEM((n_heads, 1, BLOCK), jnp.float32),
            pltpu.VMEM((n_heads, HEAD_DIM + DENOM_ROWS, BLOCK), jnp.float32),
        ],
        compiler_params=_params(2),
        name="moba_attn",
    )(qt, k5, vt, kmean)


def _attn_post_kernel(x_ref, o_ref, mod_ref, gain_ref, w_o_ref, w_gu_ref, w_down_ref, out_ref):
    mod = mod_ref[0, 0]
    y = jnp.dot(o_ref[0], w_o_ref[...], preferred_element_type=jnp.float32)
    x = x_ref[0] + mod[2:3] * y
    h = _modulated_norm(x, gain_ref[...], mod[3:4], mod[4:5]).astype(jnp.bfloat16)
    out_ref[0] = x + mod[5:6] * _swiglu(h, w_gu_ref, w_down_ref)


def _attn_post(x, o, mod, gain_ffn, w_o, w_gu, w_down, *, tm):
    batch, seq, _ = x.shape
    tile = lambda b, t: (b, t, 0)
    return pl.pallas_call(
        _attn_post_kernel,
        out_shape=jax.ShapeDtypeStruct(x.shape, x.dtype),
        grid=(batch, seq // tm),
        in_specs=[
            pl.BlockSpec((1, tm, D_MODEL), tile),
            pl.BlockSpec((1, tm, D_MODEL), tile),
            pl.BlockSpec((1, 1, 6, D_MODEL), lambda b, t: (0, b, 0, 0)),
            _const_spec((1, D_MODEL)),
            _const_spec(w_o.shape),
            _const_spec(w_gu.shape),
            _const_spec(w_down.shape),
        ],
        out_specs=pl.BlockSpec((1, tm, D_MODEL), tile),
        compiler_params=_params(2),
        name="attn_post",
    )(x, o, mod, gain_ffn, w_o, w_gu, w_down)


def _conv_layer_kernel(x_ref, mod_ref, gain_mix_ref, gain_ffn_ref, w_in_ref, conv_w_ref,
                       w_out_ref, w_gu_ref, w_down_ref, out_ref, u_ref, *, tm):
    mod = mod_ref[0, 0]
    x = x_ref[0]
    h = _modulated_norm(x, gain_mix_ref[...], mod[0:1], mod[1:2]).astype(jnp.bfloat16)

    @pl.when(pl.program_id(1) == 0)
    def _():
        u_ref[0:SUBLANES, :] = jnp.zeros((SUBLANES, D_MODEL), jnp.float32)

    @pl.when(pl.program_id(1) > 0)
    def _():
        u_ref[0:SUBLANES, :] = u_ref[tm:tm + SUBLANES, :]

    b_gate = jnp.dot(h, w_in_ref[:, :D_MODEL], preferred_element_type=jnp.float32)
    c_gate = jnp.dot(h, w_in_ref[:, D_MODEL:2 * D_MODEL], preferred_element_type=jnp.float32)
    u = c_gate * jnp.dot(h, w_in_ref[:, 2 * D_MODEL:], preferred_element_type=jnp.float32)
    u_ref[SUBLANES:, :] = u
    y = conv_w_ref[CONV_WIDTH - 1:CONV_WIDTH, :] * u
    for tap in range(CONV_WIDTH - 1):
        back = CONV_WIDTH - 1 - tap
        y = y + conv_w_ref[tap:tap + 1, :] * u_ref[SUBLANES - back:SUBLANES - back + tm, :]
    mix = jnp.dot((b_gate * y).astype(jnp.bfloat16), w_out_ref[...], preferred_element_type=jnp.float32)
    x = x + mod[2:3] * mix
    h = _modulated_norm(x, gain_ffn_ref[...], mod[3:4], mod[4:5]).astype(jnp.bfloat16)
    out_ref[0] = x + mod[5:6] * _swiglu(h, w_gu_ref, w_down_ref)


def _conv_layer(x, mod, gain_mix, gain_ffn, w_in, conv_w, w_out, w_gu, w_down, *, tm):
    batch, seq, _ = x.shape
    tile = lambda b, t: (b, t, 0)
    return pl.pallas_call(
        functools.partial(_conv_layer_kernel, tm=tm),
        out_shape=jax.ShapeDtypeStruct(x.shape, x.dtype),
        grid=(batch, seq // tm),
        in_specs=[
            pl.BlockSpec((1, tm, D_MODEL), tile),
            pl.BlockSpec((1, 1, 6, D_MODEL), lambda b, t: (1, b, 0, 0)),
            _const_spec((1, D_MODEL)),
            _const_spec((1, D_MODEL)),
            _const_spec(w_in.shape),
            _const_spec(conv_w.shape),
            _const_spec(w_out.shape),
            _const_spec(w_gu.shape),
            _const_spec(w_down.shape),
        ],
        out_specs=pl.BlockSpec((1, tm, D_MODEL), tile),
        scratch_shapes=[pltpu.VMEM((tm + SUBLANES, D_MODEL), jnp.float32)],
        compiler_params=_params(2),
        name="conv_layer",
    )(x, mod, gain_mix, gain_ffn, w_in, conv_w, w_out, w_gu, w_down)


def kernel(x, c, w_ada, b_ada, norm_mix, norm_ffn, w_qkv, w_o, q_gain, k_gain,
           w_in, conv_w, w_out, w_gate_up, w_down):
    batch, seq, _ = x.shape
    depth = w_ada.shape[0]
    assert depth == 2 and seq % BLOCK == 0
    bf16 = jnp.bfloat16
    mod = _adaln_mod(c, w_ada, b_ada).reshape(depth, batch, 6, D_MODEL)

    tm_pre, tm_ffn = 512, 512
    q, k, vt, kmean = _attn_pre(x, mod, norm_mix[0:1], w_qkv[0].astype(bf16), q_gain[0:1], k_gain[0:1],
                                tm=tm_pre)
    kmean = kmean.transpose(0, 2, 1, 3, 4).reshape(batch, N_HEADS, seq // BLOCK, HEAD_DIM)
    o = _moba_attention(q, k, vt, kmean)
    x = _attn_post(x, o, mod, norm_ffn[0:1], w_o[0].astype(bf16), w_gate_up[0].astype(bf16),
                   w_down[0].astype(bf16), tm=tm_ffn)
    x = _conv_layer(x, mod, norm_mix[1:2], norm_ffn[1:2], w_in[0].astype(bf16), conv_w[0],
                    w_out[0].astype(bf16), w_gate_up[1].astype(bf16), w_down[1].astype(bf16), tm=tm_ffn)
    return x
```

```python
import functools
import math

import jax
import jax.numpy as jnp
from jax import lax
from jax.experimental import pallas as pl
from jax.experimental.pallas import tpu as pltpu

D_MODEL = 1024
N_HEADS = 8
HEAD_DIM = D_MODEL // N_HEADS
BLOCK = 256
TOPK_BLOCKS = 3
CONV_WIDTH = 3
D_FF = int(math.ceil((8 * D_MODEL / 3) / 256) * 256)
EPS = 1e-6

MXU_COLS = 256
LANES = 128
SUBLANES = 8
DENOM_ROWS = 16
SHIFT_ROWS = 16
MAX_FIXED_SHIFT = 40.0
VMEM_LIMIT_BYTES = 56 * 1024 * 1024

MASKED = -1e30
Q_SCALE = HEAD_DIM ** -0.5 * math.log2(math.e)


def _const_spec(shape):
    zeros = (0,) * len(shape)
    return pl.BlockSpec(shape, lambda *_: zeros, pipeline_mode=pl.Buffered(1))


def _params(n_grid_axes, flags=None):
    return pltpu.CompilerParams(
        dimension_semantics=("arbitrary",) * n_grid_axes,
        vmem_limit_bytes=VMEM_LIMIT_BYTES,
        flags=flags,
    )


def _modulated_norm(x, gain, shift, scale):
    y = x * lax.rsqrt(jnp.mean(x * x, axis=-1, keepdims=True) + EPS)
    return (y * gain) * (1.0 + scale) + shift


def _ff_chunks():
    chunks, start = [], 0
    while start < D_FF:
        size = min(2 * MXU_COLS, D_FF - start)
        chunks.append((start, size))
        start += size
    return chunks


def _swiglu(h_bf16, w_gu_ref, w_down_ref):
    acc = None
    for start, size in _ff_chunks():
        g = jnp.dot(h_bf16, w_gu_ref[:, start:start + size], preferred_element_type=jnp.float32)
        u = jnp.dot(h_bf16, w_gu_ref[:, D_FF + start:D_FF + start + size],
                    preferred_element_type=jnp.float32)
        act = (g * jax.nn.sigmoid(g) * u).astype(jnp.bfloat16)
        part = jnp.dot(act, w_down_ref[start:start + size, :], preferred_element_type=jnp.float32)
        acc = part if acc is None else acc + part
    return acc


def _adaln_kernel(c_ref, w_ref, b_ref, o_ref):
    c = c_ref[...]
    sc = (c * jax.nn.sigmoid(c)).astype(jnp.bfloat16)
    w = w_ref[0].astype(jnp.bfloat16)
    o_ref[0] = jnp.dot(sc, w, preferred_element_type=jnp.float32) + b_ref[0]


def _adaln_mod(c, w_ada, b_ada):
    depth, _, n_out = w_ada.shape
    batch = c.shape[0]
    n_chunk = n_out // 4
    return pl.pallas_call(
        _adaln_kernel,
        out_shape=jax.ShapeDtypeStruct((depth, batch, n_out), jnp.float32),
        grid=(depth, n_out // n_chunk),
        in_specs=[
            pl.BlockSpec((batch, D_MODEL), lambda i, j: (0, 0)),
            pl.BlockSpec((1, D_MODEL, n_chunk), lambda i, j: (i, 0, j)),
            pl.BlockSpec((1, 1, n_chunk), lambda i, j: (i, 0, j)),
        ],
        out_specs=pl.BlockSpec((1, batch, n_chunk), lambda i, j: (i, 0, j)),
        compiler_params=_params(2),
        name="adaln_mod",
    )(c, w_ada, b_ada.reshape(depth, 1, n_out))


def _attn_pre_kernel(x_ref, mod_ref, gain_ref, w_ref, qg_ref, kg_ref,
                     qt_ref, k_ref, vt_ref, kmean_ref, *, tm):
    mod = mod_ref[0, 0]
    h = _modulated_norm(x_ref[0], gain_ref[...], mod[0:1], mod[1:2]).astype(jnp.bfloat16)
    n_blk = tm // BLOCK
    heads_per_chunk = MXU_COLS // HEAD_DIM
    for chunk in range(3 * D_MODEL // MXU_COLS):
        r = jnp.dot(h, w_ref[:, chunk * MXU_COLS:(chunk + 1) * MXU_COLS],
                    preferred_element_type=jnp.float32)
        kind = (chunk * heads_per_chunk) // N_HEADS
        for half in range(heads_per_chunk):
            head = (chunk * heads_per_chunk) % N_HEADS + half
            t = r[:, half * HEAD_DIM:(half + 1) * HEAD_DIM]
            if kind == 2:
                for blk in range(n_blk):
                    vt_ref[0, head, blk] = t[blk * BLOCK:(blk + 1) * BLOCK].T.astype(jnp.bfloat16)
                continue
            tn = t * lax.rsqrt(jnp.mean(t * t, axis=-1, keepdims=True) + EPS)
            if kind == 0:
                qn = (tn * qg_ref[...]) * Q_SCALE
                for blk in range(n_blk):
                    qt_ref[0, head, blk] = qn[blk * BLOCK:(blk + 1) * BLOCK].T.astype(jnp.bfloat16)
            else:
                kn = tn * kg_ref[...]
                k_ref[0, head] = kn.astype(jnp.bfloat16)
                kmean_ref[0, 0, head] = jnp.mean(kn.reshape(n_blk, BLOCK, HEAD_DIM), axis=1)


def _attn_pre(x, mod, gain, w_qkv_bf16, q_gain, k_gain, *, tm):
    batch, seq, _ = x.shape
    n_tiles = seq // tm
    n_blk = tm // BLOCK
    t_shape = jax.ShapeDtypeStruct((batch, N_HEADS, seq // BLOCK, HEAD_DIM, BLOCK), jnp.bfloat16)
    t_spec = pl.BlockSpec((1, N_HEADS, n_blk, HEAD_DIM, BLOCK), lambda b, t: (b, 0, t, 0, 0))
    return pl.pallas_call(
        functools.partial(_attn_pre_kernel, tm=tm),
        out_shape=(
            t_shape,
            jax.ShapeDtypeStruct((batch, N_HEADS, seq, HEAD_DIM), jnp.bfloat16),
            t_shape,
            jax.ShapeDtypeStruct((batch, n_tiles, N_HEADS, n_blk, HEAD_DIM), jnp.float32),
        ),
        grid=(batch, n_tiles),
        in_specs=[
            pl.BlockSpec((1, tm, D_MODEL), lambda b, t: (b, t, 0)),
            pl.BlockSpec((1, 1, 6, D_MODEL), lambda b, t: (0, b, 0, 0)),
            _const_spec((1, D_MODEL)),
            _const_spec((D_MODEL, 3 * D_MODEL)),
            _const_spec((1, HEAD_DIM)),
            _const_spec((1, HEAD_DIM)),
        ],
        out_specs=(
            t_spec,
            pl.BlockSpec((1, N_HEADS, tm, HEAD_DIM), lambda b, t: (b, 0, t, 0)),
            t_spec,
            pl.BlockSpec((1, 1, N_HEADS, n_blk, HEAD_DIM), lambda b, t: (b, t, 0, 0, 0)),
        ),
        compiler_params=_params(2),
        name="attn_pre",
    )(x, mod, gain, w_qkv_bf16, q_gain, k_gain)


def _moba_kernel(qt_ref, k_ref, vt_ref, kmean_ref, shift_ref, o_ref,
                 q2t_ref, stage_ref, acc_ref, *stat_refs, n_blocks, fixed_shift):
    nb = pl.program_id(1)
    blk_id = lax.broadcasted_iota(jnp.int32, (n_blocks, BLOCK), 0).astype(jnp.float32)
    key_pos = lax.broadcasted_iota(jnp.int32, (BLOCK, BLOCK), 0)
    qry_pos = lax.broadcasted_iota(jnp.int32, (BLOCK, BLOCK), 1)
    lane_id = lax.broadcasted_iota(jnp.int32, (BLOCK, LANES), 1)
    neg_inf = jnp.float32(-jnp.inf)
    zero_rows = jnp.zeros((BLOCK - HEAD_DIM - n_blocks - SHIFT_ROWS, BLOCK), jnp.bfloat16)
    ones_rows = jnp.ones((DENOM_ROWS, BLOCK), jnp.bfloat16)
    if not fixed_shift:
        mb_ref, m_ref = stat_refs

    for h in range(N_HEADS):
        qt = qt_ref[0, h, 0]
        kmean = kmean_ref[0, h]
        km_hi = kmean.astype(jnp.bfloat16)
        km_lo = (kmean - km_hi.astype(jnp.float32)).astype(jnp.bfloat16)
        r2 = jnp.dot(jnp.concatenate([km_hi, km_lo], axis=0), qt, preferred_element_type=jnp.float32)
        route = jnp.where(blk_id < nb.astype(jnp.float32), r2[:n_blocks] + r2[n_blocks:], neg_inf)
        bias = jnp.full((n_blocks, BLOCK), MASKED, jnp.float32)
        for _ in range(TOPK_BLOCKS):
            best = jnp.max(route, axis=0, keepdims=True)
            first = jnp.min(jnp.where(route == best, blk_id, float(n_blocks)), axis=0, keepdims=True)
            pick = blk_id == first
            bias = jnp.where(pick & (best > neg_inf), 0.0, bias)
            route = jnp.where(pick, neg_inf, route)
        extra = jnp.concatenate([bias, shift_ref[...]], axis=0).astype(jnp.bfloat16)
        q2t_ref[h] = jnp.concatenate([qt, extra, zero_rows], axis=0)

    def key_extension(j):
        hit = None if j is None else lane_id == j
        if fixed_shift:
            hit = lane_id == n_blocks if hit is None else hit | (lane_id == n_blocks)
        if hit is None:
            return jnp.zeros((BLOCK, LANES), jnp.bfloat16)
        return jnp.where(hit, 1.0, 0.0).astype(jnp.bfloat16)

    def produce(h, j, extension, slot, own):
        keys = jnp.concatenate([k_ref[0, h, j], extension], axis=1)
        s = jnp.dot(keys, q2t_ref[h], preferred_element_type=jnp.float32)
        if own:
            s = jnp.where(key_pos <= qry_pos, s, MASKED)
        if fixed_shift:
            stage_ref[slot, h] = jnp.exp2(s).astype(jnp.bfloat16)
        else:
            stage_ref[slot, h] = s
            mb_ref[slot, h] = jnp.max(s, axis=0, keepdims=True)

    def consume(h, j, slot, first):
        values = jnp.concatenate([vt_ref[0, h, j], ones_rows], axis=0)
        if fixed_shift:
            pv = jnp.dot(values, stage_ref[slot, h], preferred_element_type=jnp.float32)
            acc_ref[h] = pv if first else acc_ref[h] + pv
            return
        m_blk = mb_ref[slot, h]
        if first:
            m_new = m_blk
        else:
            m = m_ref[h]
            m_new = jnp.maximum(m, m_blk)
            alpha = jnp.exp2(m - m_new)
        p = jnp.exp2(stage_ref[slot, h] - m_new).astype(jnp.bfloat16)
        pv = jnp.dot(values, p, preferred_element_type=jnp.float32)
        m_ref[h] = m_new
        acc_ref[h] = pv if first else alpha * acc_ref[h] + pv

    def step(j, slot, j_next):
        extension = key_extension(j_next)
        for h in range(N_HEADS):
            produce(h, j_next, extension, 1 - slot, False)
            consume(h, j, slot, False)

    extension = key_extension(None)
    for h in range(N_HEADS):
        produce(h, nb, extension, 1, True)
    extension = key_extension(0)
    for h in range(N_HEADS):
        produce(h, 0, extension, 0, False)
        consume(h, nb, 1, True)

    def key_block_pair(i, _):
        j = 2 * i
        step(j, 0, j + 1)
        step(j + 1, 1, jnp.minimum(j + 2, n_blocks - 1))
        return 0

    lax.fori_loop(0, lax.shift_right_logical(nb + 1, 1), key_block_pair, 0)
    for h in range(N_HEADS):
        out_t = acc_ref[h, 0:HEAD_DIM, :] / acc_ref[h, HEAD_DIM:HEAD_DIM + 1, :]
        o_ref[0, :, h * HEAD_DIM:(h + 1) * HEAD_DIM] = out_t.T.astype(o_ref.dtype)


def _moba_attention(qt, k, vt, kmean, shift_rows, *, fixed_shift):
    batch, n_heads, n_blocks, _, _ = qt.shape
    seq = n_blocks * BLOCK
    k5 = k.reshape(batch, n_heads, n_blocks, BLOCK, HEAD_DIM)
    per_batch = lambda b, nb: (b,) + (0,) * 4
    stage_dtype = jnp.bfloat16 if fixed_shift else jnp.float32
    scratch = [
        pltpu.VMEM((n_heads, BLOCK, BLOCK), jnp.bfloat16),
        pltpu.VMEM((2, n_heads, BLOCK, BLOCK), stage_dtype),
        pltpu.VMEM((n_heads, HEAD_DIM + DENOM_ROWS, BLOCK), jnp.float32),
    ]
    if not fixed_shift:
        scratch += [
            pltpu.VMEM((2, n_heads, 1, BLOCK), jnp.float32),
            pltpu.VMEM((n_heads, 1, BLOCK), jnp.float32),
        ]
    return pl.pallas_call(
        functools.partial(_moba_kernel, n_blocks=n_blocks, fixed_shift=fixed_shift),
        out_shape=jax.ShapeDtypeStruct((batch, seq, D_MODEL), jnp.bfloat16),
        grid=(batch, n_blocks),
        in_specs=[
            pl.BlockSpec((1, n_heads, 1, HEAD_DIM, BLOCK), lambda b, nb: (b, 0, nb, 0, 0)),
            pl.BlockSpec((1, n_heads, n_blocks, BLOCK, HEAD_DIM), per_batch, pipeline_mode=pl.Buffered(1)),
            pl.BlockSpec((1, n_heads, n_blocks, HEAD_DIM, BLOCK), per_batch, pipeline_mode=pl.Buffered(1)),
            pl.BlockSpec((1, n_heads, n_blocks, HEAD_DIM), lambda b, nb: (b, 0, 0, 0)),
            _const_spec(shift_rows.shape),
        ],
        out_specs=pl.BlockSpec((1, BLOCK, D_MODEL), lambda b, nb: (b, nb, 0)),
        scratch_shapes=scratch,
        compiler_params=_params(2),
        name="moba_attn_fixed_shift" if fixed_shift else "moba_attn_online",
    )(qt, k5, vt, kmean, shift_rows)


def _attn_post_kernel(x_ref, o_ref, mod_ref, gain_ref, w_o_ref, w_gu_ref, w_down_ref, out_ref):
    mod = mod_ref[0, 0]
    y = jnp.dot(o_ref[0], w_o_ref[...], preferred_element_type=jnp.float32)
    x = x_ref[0] + mod[2:3] * y
    h = _modulated_norm(x, gain_ref[...], mod[3:4], mod[4:5]).astype(jnp.bfloat16)
    out_ref[0] = x + mod[5:6] * _swiglu(h, w_gu_ref, w_down_ref)


def _attn_post(x, o, mod, gain_ffn, w_o, w_gu, w_down, *, tm):
    batch, seq, _ = x.shape
    tile = lambda b, t: (b, t, 0)
    return pl.pallas_call(
        _attn_post_kernel,
        out_shape=jax.ShapeDtypeStruct(x.shape, x.dtype),
        grid=(batch, seq // tm),
        in_specs=[
            pl.BlockSpec((1, tm, D_MODEL), tile),
            pl.BlockSpec((1, tm, D_MODEL), tile),
            pl.BlockSpec((1, 1, 6, D_MODEL), lambda b, t: (0, b, 0, 0)),
            _const_spec((1, D_MODEL)),
            _const_spec(w_o.shape),
            _const_spec(w_gu.shape),
            _const_spec(w_down.shape),
        ],
        out_specs=pl.BlockSpec((1, tm, D_MODEL), tile),
        compiler_params=_params(2),
        name="attn_post",
    )(x, o, mod, gain_ffn, w_o, w_gu, w_down)


def _conv_layer_kernel(x_ref, mod_ref, gain_mix_ref, gain_ffn_ref, w_in_ref, conv_w_ref,
                       w_out_ref, w_gu_ref, w_down_ref, out_ref, u_ref, *, tm):
    mod = mod_ref[0, 0]
    x = x_ref[0]
    h = _modulated_norm(x, gain_mix_ref[...], mod[0:1], mod[1:2]).astype(jnp.bfloat16)

    @pl.when(pl.program_id(1) == 0)
    def _():
        u_ref[0:SUBLANES, :] = jnp.zeros((SUBLANES, D_MODEL), jnp.float32)

    @pl.when(pl.program_id(1) > 0)
    def _():
        u_ref[0:SUBLANES, :] = u_ref[tm:tm + SUBLANES, :]

    b_gate = jnp.dot(h, w_in_ref[:, :D_MODEL], preferred_element_type=jnp.float32)
    c_gate = jnp.dot(h, w_in_ref[:, D_MODEL:2 * D_MODEL], preferred_element_type=jnp.float32)
    u = c_gate * jnp.dot(h, w_in_ref[:, 2 * D_MODEL:], preferred_element_type=jnp.float32)
    u_ref[SUBLANES:, :] = u
    y = conv_w_ref[CONV_WIDTH - 1:CONV_WIDTH, :] * u
    for tap in range(CONV_WIDTH - 1):
        back = CONV_WIDTH - 1 - tap
        y = y + conv_w_ref[tap:tap + 1, :] * u_ref[SUBLANES - back:SUBLANES - back + tm, :]
    mix = jnp.dot((b_gate * y).astype(jnp.bfloat16), w_out_ref[...], preferred_element_type=jnp.float32)
    x = x + mod[2:3] * mix
    h = _modulated_norm(x, gain_ffn_ref[...], mod[3:4], mod[4:5]).astype(jnp.bfloat16)
    out_ref[0] = x + mod[5:6] * _swiglu(h, w_gu_ref, w_down_ref)


def _conv_layer(x, mod, gain_mix, gain_ffn, w_in, conv_w, w_out, w_gu, w_down, *, tm):
    batch, seq, _ = x.shape
    tile = lambda b, t: (b, t, 0)
    return pl.pallas_call(
        functools.partial(_conv_layer_kernel, tm=tm),
        out_shape=jax.ShapeDtypeStruct(x.shape, x.dtype),
        grid=(batch, seq // tm),
        in_specs=[
            pl.BlockSpec((1, tm, D_MODEL), tile),
            pl.BlockSpec((1, 1, 6, D_MODEL), lambda b, t: (1, b, 0, 0)),
            _const_spec((1, D_MODEL)),
            _const_spec((1, D_MODEL)),
            _const_spec(w_in.shape),
            _const_spec(conv_w.shape),
            _const_spec(w_out.shape),
            _const_spec(w_gu.shape),
            _const_spec(w_down.shape),
        ],
        out_specs=pl.BlockSpec((1, tm, D_MODEL), tile),
        scratch_shapes=[pltpu.VMEM((tm + SUBLANES, D_MODEL), jnp.float32)],
        compiler_params=_params(2),
        name="conv_layer",
    )(x, mod, gain_mix, gain_ffn, w_in, conv_w, w_out, w_gu, w_down)


def kernel(x, c, w_ada, b_ada, norm_mix, norm_ffn, w_qkv, w_o, q_gain, k_gain,
           w_in, conv_w, w_out, w_gate_up, w_down):
    batch, seq, _ = x.shape
    depth = w_ada.shape[0]
    assert depth == 2 and seq % BLOCK == 0
    bf16 = jnp.bfloat16
    mod = _adaln_mod(c, w_ada, b_ada).reshape(depth, batch, 6, D_MODEL)

    tm_pre, tm_ffn = 512, 512
    q, k, vt, kmean = _attn_pre(x, mod, norm_mix[0:1], w_qkv[0].astype(bf16), q_gain[0:1], k_gain[0:1],
                                tm=tm_pre)
    kmean = kmean.transpose(0, 2, 1, 3, 4).reshape(batch, N_HEADS, seq // BLOCK, HEAD_DIM)
    score_bound = HEAD_DIM * Q_SCALE * jnp.max(jnp.abs(q_gain[0])) * jnp.max(jnp.abs(k_gain[0]))
    shift_rows = jnp.zeros((SHIFT_ROWS, BLOCK), jnp.float32).at[0].set(-score_bound)
    o = lax.cond(
        score_bound <= MAX_FIXED_SHIFT,
        functools.partial(_moba_attention, fixed_shift=True),
        functools.partial(_moba_attention, fixed_shift=False),
        q, k, vt, kmean, shift_rows)
    x = _attn_post(x, o, mod, norm_ffn[0:1], w_o[0].astype(bf16), w_gate_up[0].astype(bf16),
                   w_down[0].astype(bf16), tm=tm_ffn)
    x = _conv_layer(x, mod, norm_mix[1:2], norm_ffn[1:2], w_in[0].astype(bf16), conv_w[0],
                    w_out[0].astype(bf16), w_gate_up[1].astype(bf16), w_down[1].astype(bf16), tm=tm_ffn)
    return x
```

```python
import functools
import math

import jax
import jax.numpy as jnp
from jax import lax
from jax.experimental import pallas as pl
from jax.experimental.pallas import tpu as pltpu

D_MODEL = 1024
N_HEADS = 8
HEAD_DIM = D_MODEL // N_HEADS
BLOCK = 256
TOPK_BLOCKS = 3
CONV_WIDTH = 3
D_FF = int(math.ceil((8 * D_MODEL / 3) / 256) * 256)
EPS = 1e-6

MXU_COLS = 256
LANES = 128
SUBLANES = 8
DENOM_ROWS = 16
SHIFT_ROWS = 16
MAX_FIXED_SHIFT = 40.0
VMEM_LIMIT_BYTES = 56 * 1024 * 1024

MASKED = -1e30
Q_SCALE = HEAD_DIM ** -0.5 * math.log2(math.e)


def _const_spec(shape):
    zeros = (0,) * len(shape)
    return pl.BlockSpec(shape, lambda *_: zeros, pipeline_mode=pl.Buffered(1))


def _params(n_grid_axes, flags=None):
    return pltpu.CompilerParams(
        dimension_semantics=("arbitrary",) * n_grid_axes,
        vmem_limit_bytes=VMEM_LIMIT_BYTES,
        flags=flags,
    )


def _modulated_norm(x, gain, shift, scale):
    y = x * lax.rsqrt(jnp.mean(x * x, axis=-1, keepdims=True) + EPS)
    return (y * gain) * (1.0 + scale) + shift


def _ff_chunks():
    chunks, start = [], 0
    while start < D_FF:
        size = min(2 * MXU_COLS, D_FF - start)
        chunks.append((start, size))
        start += size
    return chunks


def _swiglu(h_bf16, w_gu_ref, w_down_ref):
    acc = None
    for start, size in _ff_chunks():
        g = jnp.dot(h_bf16, w_gu_ref[:, start:start + size], preferred_element_type=jnp.float32)
        u = jnp.dot(h_bf16, w_gu_ref[:, D_FF + start:D_FF + start + size],
                    preferred_element_type=jnp.float32)
        act = (g * jax.nn.sigmoid(g) * u).astype(jnp.bfloat16)
        part = jnp.dot(act, w_down_ref[start:start + size, :], preferred_element_type=jnp.float32)
        acc = part if acc is None else acc + part
    return acc


def _adaln_kernel(c_ref, w_ref, b_ref, o_ref):
    c = c_ref[...]
    sc = (c * jax.nn.sigmoid(c)).astype(jnp.bfloat16)
    w = w_ref[0].astype(jnp.bfloat16)
    o_ref[0] = jnp.dot(sc, w, preferred_element_type=jnp.float32) + b_ref[0]


def _adaln_mod(c, w_ada, b_ada):
    depth, _, n_out = w_ada.shape
    batch = c.shape[0]
    n_chunk = n_out // 4
    return pl.pallas_call(
        _adaln_kernel,
        out_shape=jax.ShapeDtypeStruct((depth, batch, n_out), jnp.float32),
        grid=(depth, n_out // n_chunk),
        in_specs=[
            pl.BlockSpec((batch, D_MODEL), lambda i, j: (0, 0)),
            pl.BlockSpec((1, D_MODEL, n_chunk), lambda i, j: (i, 0, j)),
            pl.BlockSpec((1, 1, n_chunk), lambda i, j: (i, 0, j)),
        ],
        out_specs=pl.BlockSpec((1, batch, n_chunk), lambda i, j: (i, 0, j)),
        compiler_params=_params(2),
        name="adaln_mod",
    )(c, w_ada, b_ada.reshape(depth, 1, n_out))


def _attn_pre_kernel(x_ref, mod_ref, gain_ref, w_ref, qg_ref, kg_ref,
                     qt_ref, k_ref, vt_ref, kmean_ref, *, tm):
    mod = mod_ref[0, 0]
    h = _modulated_norm(x_ref[0], gain_ref[...], mod[0:1], mod[1:2]).astype(jnp.bfloat16)
    n_blk = tm // BLOCK
    heads_per_chunk = MXU_COLS // HEAD_DIM
    for chunk in range(3 * D_MODEL // MXU_COLS):
        r = jnp.dot(h, w_ref[:, chunk * MXU_COLS:(chunk + 1) * MXU_COLS],
                    preferred_element_type=jnp.float32)
        kind = (chunk * heads_per_chunk) // N_HEADS
        for half in range(heads_per_chunk):
            head = (chunk * heads_per_chunk) % N_HEADS + half
            t = r[:, half * HEAD_DIM:(half + 1) * HEAD_DIM]
            if kind == 2:
                for blk in range(n_blk):
                    vt_ref[0, head, blk] = t[blk * BLOCK:(blk + 1) * BLOCK].T.astype(jnp.bfloat16)
                continue
            tn = t * lax.rsqrt(jnp.mean(t * t, axis=-1, keepdims=True) + EPS)
            if kind == 0:
                qn = (tn * qg_ref[...]) * Q_SCALE
                for blk in range(n_blk):
                    qt_ref[0, head, blk] = qn[blk * BLOCK:(blk + 1) * BLOCK].T.astype(jnp.bfloat16)
            else:
                kn = tn * kg_ref[...]
                k_ref[0, head] = kn.astype(jnp.bfloat16)
                kmean_ref[0, 0, head] = jnp.mean(kn.reshape(n_blk, BLOCK, HEAD_DIM), axis=1)


def _attn_pre(x, mod, gain, w_qkv_bf16, q_gain, k_gain, *, tm):
    batch, seq, _ = x.shape
    n_tiles = seq // tm
    n_blk = tm // BLOCK
    t_shape = jax.ShapeDtypeStruct((batch, N_HEADS, seq // BLOCK, HEAD_DIM, BLOCK), jnp.bfloat16)
    t_spec = pl.BlockSpec((1, N_HEADS, n_blk, HEAD_DIM, BLOCK), lambda b, t: (b, 0, t, 0, 0))
    return pl.pallas_call(
        functools.partial(_attn_pre_kernel, tm=tm),
        out_shape=(
            t_shape,
            jax.ShapeDtypeStruct((batch, N_HEADS, seq, HEAD_DIM), jnp.bfloat16),
            t_shape,
            jax.ShapeDtypeStruct((batch, n_tiles, N_HEADS, n_blk, HEAD_DIM), jnp.float32),
        ),
        grid=(batch, n_tiles),
        in_specs=[
            pl.BlockSpec((1, tm, D_MODEL), lambda b, t: (b, t, 0)),
            pl.BlockSpec((1, 1, 6, D_MODEL), lambda b, t: (0, b, 0, 0)),
            _const_spec((1, D_MODEL)),
            _const_spec((D_MODEL, 3 * D_MODEL)),
            _const_spec((1, HEAD_DIM)),
            _const_spec((1, HEAD_DIM)),
        ],
        out_specs=(
            t_spec,
            pl.BlockSpec((1, N_HEADS, tm, HEAD_DIM), lambda b, t: (b, 0, t, 0)),
            t_spec,
            pl.BlockSpec((1, 1, N_HEADS, n_blk, HEAD_DIM), lambda b, t: (b, t, 0, 0, 0)),
        ),
        compiler_params=_params(2),
        name="attn_pre",
    )(x, mod, gain, w_qkv_bf16, q_gain, k_gain)


def _moba_kernel(qt_ref, k_ref, vt_ref, kmean_ref, shift_ref, o_ref,
                 q2t_ref, stage_ref, acc_ref, *stat_refs, n_blocks, fixed_shift):
    nb = pl.program_id(1)
    blk_id = lax.broadcasted_iota(jnp.int32, (n_blocks, BLOCK), 0).astype(jnp.float32)
    key_pos = lax.broadcasted_iota(jnp.int32, (BLOCK, BLOCK), 0)
    qry_pos = lax.broadcasted_iota(jnp.int32, (BLOCK, BLOCK), 1)
    lane_id = lax.broadcasted_iota(jnp.int32, (BLOCK, LANES), 1)
    neg_inf = jnp.float32(-jnp.inf)
    zero_rows = jnp.zeros((BLOCK - HEAD_DIM - n_blocks - SHIFT_ROWS, BLOCK), jnp.bfloat16)
    ones_rows = jnp.ones((DENOM_ROWS, BLOCK), jnp.bfloat16)
    if fixed_shift:
        (l_ref,) = stat_refs
    else:
        mb_ref, m_ref = stat_refs

    def route_scores(h):
        kmean = kmean_ref[0, h]
        km_hi = kmean.astype(jnp.bfloat16)
        km_lo = (kmean - km_hi.astype(jnp.float32)).astype(jnp.bfloat16)
        r2 = jnp.dot(jnp.concatenate([km_hi, km_lo], axis=0), qt_ref[0, h, 0],
                     preferred_element_type=jnp.float32)
        return r2[:n_blocks] + r2[n_blocks:]

    def route_mask(score):
        score = jnp.where(blk_id < nb.astype(jnp.float32), score, neg_inf)
        mask = jnp.full((n_blocks, BLOCK), MASKED, jnp.float32)
        for _ in range(TOPK_BLOCKS):
            best = jnp.max(score, axis=0, keepdims=True)
            first = jnp.min(jnp.where(score == best, blk_id, float(n_blocks)), axis=0, keepdims=True)
            pick = blk_id == first
            mask = jnp.where(pick & (best > neg_inf), 0.0, mask)
            score = jnp.where(pick, neg_inf, score)
        return mask

    def key_extension(j):
        hit = None if j is None else lane_id == j
        if fixed_shift:
            hit = lane_id == n_blocks if hit is None else hit | (lane_id == n_blocks)
        if hit is None:
            return jnp.zeros((BLOCK, LANES), jnp.bfloat16)
        return jnp.where(hit, 1.0, 0.0).astype(jnp.bfloat16)

    def produce(h, j, extension, slot, own):
        keys = jnp.concatenate([k_ref[0, h, j], extension], axis=1)
        s = jnp.dot(keys, q2t_ref[h], preferred_element_type=jnp.float32)
        if own:
            s = jnp.where(key_pos <= qry_pos, s, MASKED)
        if fixed_shift:
            p = jnp.exp2(s)
            p_sum = jnp.sum(p, axis=0, keepdims=True)
            l_ref[h] = p_sum if own else l_ref[h] + p_sum
            stage_ref[slot, h] = p.astype(jnp.bfloat16)
        else:
            stage_ref[slot, h] = s
            mb_ref[slot, h] = jnp.max(s, axis=0, keepdims=True)

    def consume(h, j, slot, first):
        if fixed_shift:
            pv = jnp.dot(vt_ref[0, h, j], stage_ref[slot, h], preferred_element_type=jnp.float32)
            acc_ref[h] = pv if first else acc_ref[h] + pv
            return
        values = jnp.concatenate([vt_ref[0, h, j], ones_rows], axis=0)
        m_blk = mb_ref[slot, h]
        if first:
            m_new = m_blk
        else:
            m = m_ref[h]
            m_new = jnp.maximum(m, m_blk)
            alpha = jnp.exp2(m - m_new)
        p = jnp.exp2(stage_ref[slot, h] - m_new).astype(jnp.bfloat16)
        pv = jnp.dot(values, p, preferred_element_type=jnp.float32)
        m_ref[h] = m_new
        acc_ref[h] = pv if first else alpha * acc_ref[h] + pv

    def step(j, slot, j_next):
        extension = key_extension(j_next)
        for h in range(N_HEADS):
            produce(h, j_next, extension, 1 - slot, False)
            consume(h, j, slot, False)

    unrouted = jnp.concatenate([jnp.zeros((n_blocks, BLOCK), jnp.float32), shift_ref[...]],
                               axis=0).astype(jnp.bfloat16)
    for h in range(N_HEADS):
        q2t_ref[h] = jnp.concatenate([qt_ref[0, h, 0], unrouted, zero_rows], axis=0)
    routing_scores = [route_scores(h) for h in range(N_HEADS)]
    extension = key_extension(None)
    for h in range(N_HEADS):
        produce(h, nb, extension, 1, True)
    for h in range(N_HEADS):
        q2t_ref[h, HEAD_DIM:HEAD_DIM + n_blocks, :] = route_mask(routing_scores[h]).astype(jnp.bfloat16)
    extension = key_extension(0)
    for h in range(N_HEADS):
        produce(h, 0, extension, 0, False)
        consume(h, nb, 1, True)

    def key_block_pair(i, _):
        j = 2 * i
        step(j, 0, j + 1)
        step(j + 1, 1, jnp.minimum(j + 2, n_blocks - 1))
        return 0

    lax.fori_loop(0, lax.shift_right_logical(nb + 1, 1), key_block_pair, 0)
    for h in range(N_HEADS):
        denom = l_ref[h] if fixed_shift else acc_ref[h, HEAD_DIM:HEAD_DIM + 1, :]
        out_t = acc_ref[h, 0:HEAD_DIM, :] / denom
        o_ref[0, :, h * HEAD_DIM:(h + 1) * HEAD_DIM] = out_t.T.astype(o_ref.dtype)


def _moba_attention(qt, k, vt, kmean, shift_rows, *, fixed_shift):
    batch, n_heads, n_blocks, _, _ = qt.shape
    seq = n_blocks * BLOCK
    k5 = k.reshape(batch, n_heads, n_blocks, BLOCK, HEAD_DIM)
    per_batch = lambda b, nb: (b,) + (0,) * 4
    q2t_scratch = pltpu.VMEM((n_heads, BLOCK, BLOCK), jnp.bfloat16)
    if fixed_shift:
        scratch = [
            q2t_scratch,
            pltpu.VMEM((2, n_heads, BLOCK, BLOCK), jnp.bfloat16),
            pltpu.VMEM((n_heads, HEAD_DIM, BLOCK), jnp.float32),
            pltpu.VMEM((n_heads, 1, BLOCK), jnp.float32),
        ]
    else:
        scratch = [
            q2t_scratch,
            pltpu.VMEM((2, n_heads, BLOCK, BLOCK), jnp.float32),
            pltpu.VMEM((n_heads, HEAD_DIM + DENOM_ROWS, BLOCK), jnp.float32),
            pltpu.VMEM((2, n_heads, 1, BLOCK), jnp.float32),
            pltpu.VMEM((n_heads, 1, BLOCK), jnp.float32),
        ]
    return pl.pallas_call(
        functools.partial(_moba_kernel, n_blocks=n_blocks, fixed_shift=fixed_shift),
        out_shape=jax.ShapeDtypeStruct((batch, seq, D_MODEL), jnp.bfloat16),
        grid=(batch, n_blocks),
        in_specs=[
            pl.BlockSpec((1, n_heads, 1, HEAD_DIM, BLOCK), lambda b, nb: (b, 0, nb, 0, 0)),
            pl.BlockSpec((1, n_heads, n_blocks, BLOCK, HEAD_DIM), per_batch, pipeline_mode=pl.Buffered(1)),
            pl.BlockSpec((1, n_heads, n_blocks, HEAD_DIM, BLOCK), per_batch, pipeline_mode=pl.Buffered(1)),
            pl.BlockSpec((1, n_heads, n_blocks, HEAD_DIM), lambda b, nb: (b, 0, 0, 0)),
            _const_spec(shift_rows.shape),
        ],
        out_specs=pl.BlockSpec((1, BLOCK, D_MODEL), lambda b, nb: (b, nb, 0)),
        scratch_shapes=scratch,
        compiler_params=_params(2),
        name="moba_attn_fixed_shift" if fixed_shift else "moba_attn_online",
    )(qt, k5, vt, kmean, shift_rows)


def _attn_post_kernel(x_ref, o_ref, mod_ref, gain_ref, w_o_ref, w_gu_ref, w_down_ref, out_ref):
    mod = mod_ref[0, 0]
    y = jnp.dot(o_ref[0], w_o_ref[...], preferred_element_type=jnp.float32)
    x = x_ref[0] + mod[2:3] * y
    h = _modulated_norm(x, gain_ref[...], mod[3:4], mod[4:5]).astype(jnp.bfloat16)
    out_ref[0] = x + mod[5:6] * _swiglu(h, w_gu_ref, w_down_ref)


def _attn_post(x, o, mod, gain_ffn, w_o, w_gu, w_down, *, tm):
    batch, seq, _ = x.shape
    tile = lambda b, t: (b, t, 0)
    return pl.pallas_call(
        _attn_post_kernel,
        out_shape=jax.ShapeDtypeStruct(x.shape, x.dtype),
        grid=(batch, seq // tm),
        in_specs=[
            pl.BlockSpec((1, tm, D_MODEL), tile),
            pl.BlockSpec((1, tm, D_MODEL), tile),
            pl.BlockSpec((1, 1, 6, D_MODEL), lambda b, t: (0, b, 0, 0)),
            _const_spec((1, D_MODEL)),
            _const_spec(w_o.shape),
            _const_spec(w_gu.shape),
            _const_spec(w_down.shape),
        ],
        out_specs=pl.BlockSpec((1, tm, D_MODEL), tile),
        compiler_params=_params(2),
        name="attn_post",
    )(x, o, mod, gain_ffn, w_o, w_gu, w_down)


def _conv_layer_kernel(x_ref, mod_ref, gain_mix_ref, gain_ffn_ref, w_in_ref, conv_w_ref,
                       w_out_ref, w_gu_ref, w_down_ref, out_ref, u_ref, *, tm):
    mod = mod_ref[0, 0]
    x = x_ref[0]
    h = _modulated_norm(x, gain_mix_ref[...], mod[0:1], mod[1:2]).astype(jnp.bfloat16)

    @pl.when(pl.program_id(1) == 0)
    def _():
        u_ref[0:SUBLANES, :] = jnp.zeros((SUBLANES, D_MODEL), jnp.float32)

    @pl.when(pl.program_id(1) > 0)
    def _():
        u_ref[0:SUBLANES, :] = u_ref[tm:tm + SUBLANES, :]

    b_gate = jnp.dot(h, w_in_ref[:, :D_MODEL], preferred_element_type=jnp.float32)
    c_gate = jnp.dot(h, w_in_ref[:, D_MODEL:2 * D_MODEL], preferred_element_type=jnp.float32)
    u = c_gate * jnp.dot(h, w_in_ref[:, 2 * D_MODEL:], preferred_element_type=jnp.float32)
    u_ref[SUBLANES:, :] = u
    y = conv_w_ref[CONV_WIDTH - 1:CONV_WIDTH, :] * u
    for tap in range(CONV_WIDTH - 1):
        back = CONV_WIDTH - 1 - tap
        y = y + conv_w_ref[tap:tap + 1, :] * u_ref[SUBLANES - back:SUBLANES - back + tm, :]
    mix = jnp.dot((b_gate * y).astype(jnp.bfloat16), w_out_ref[...], preferred_element_type=jnp.float32)
    x = x + mod[2:3] * mix
    h = _modulated_norm(x, gain_ffn_ref[...], mod[3:4], mod[4:5]).astype(jnp.bfloat16)
    out_ref[0] = x + mod[5:6] * _swiglu(h, w_gu_ref, w_down_ref)


def _conv_layer(x, mod, gain_mix, gain_ffn, w_in, conv_w, w_out, w_gu, w_down, *, tm):
    batch, seq, _ = x.shape
    tile = lambda b, t: (b, t, 0)
    return pl.pallas_call(
        functools.partial(_conv_layer_kernel, tm=tm),
        out_shape=jax.ShapeDtypeStruct(x.shape, x.dtype),
        grid=(batch, seq // tm),
        in_specs=[
            pl.BlockSpec((1, tm, D_MODEL), tile),
            pl.BlockSpec((1, 1, 6, D_MODEL), lambda b, t: (1, b, 0, 0)),
            _const_spec((1, D_MODEL)),
            _const_spec((1, D_MODEL)),
            _const_spec(w_in.shape),
            _const_spec(conv_w.shape),
            _const_spec(w_out.shape),
            _const_spec(w_gu.shape),
            _const_spec(w_down.shape),
        ],
        out_specs=pl.BlockSpec((1, tm, D_MODEL), tile),
        scratch_shapes=[pltpu.VMEM((tm + SUBLANES, D_MODEL), jnp.float32)],
        compiler_params=_params(2),
        name="conv_layer",
    )(x, mod, gain_mix, gain_ffn, w_in, conv_w, w_out, w_gu, w_down)


def kernel(x, c, w_ada, b_ada, norm_mix, norm_ffn, w_qkv, w_o, q_gain, k_gain,
           w_in, conv_w, w_out, w_gate_up, w_down):
    batch, seq, _ = x.shape
    depth = w_ada.shape[0]
    assert depth == 2 and seq % BLOCK == 0
    bf16 = jnp.bfloat16
    mod = _adaln_mod(c, w_ada, b_ada).reshape(depth, batch, 6, D_MODEL)

    tm_pre, tm_ffn = 512, 512
    q, k, vt, kmean = _attn_pre(x, mod, norm_mix[0:1], w_qkv[0].astype(bf16), q_gain[0:1], k_gain[0:1],
                                tm=tm_pre)
    kmean = kmean.transpose(0, 2, 1, 3, 4).reshape(batch, N_HEADS, seq // BLOCK, HEAD_DIM)
    score_bound = HEAD_DIM * Q_SCALE * jnp.max(jnp.abs(q_gain[0])) * jnp.max(jnp.abs(k_gain[0]))
    shift_rows = jnp.zeros((SHIFT_ROWS, BLOCK), jnp.float32).at[0].set(-score_bound)
    o = lax.cond(
        score_bound <= MAX_FIXED_SHIFT,
        functools.partial(_moba_attention, fixed_shift=True),
        functools.partial(_moba_attention, fixed_shift=False),
        q, k, vt, kmean, shift_rows)
    x = _attn_post(x, o, mod, norm_ffn[0:1], w_o[0].astype(bf16), w_gate_up[0].astype(bf16),
                   w_down[0].astype(bf16), tm=tm_ffn)
    x = _conv_layer(x, mod, norm_mix[1:2], norm_ffn[1:2], w_in[0].astype(bf16), conv_w[0],
                    w_out[0].astype(bf16), w_gate_up[1].astype(bf16), w_down[1].astype(bf16), tm=tm_ffn)
    return x
```

```python
import functools
import math

import jax
import jax.numpy as jnp
from jax import lax
from jax.experimental import pallas as pl
from jax.experimental.pallas import tpu as pltpu

D_MODEL = 1024
N_HEADS = 8
HEAD_DIM = D_MODEL // N_HEADS
BLOCK = 256
TOPK_BLOCKS = 3
CONV_WIDTH = 3
D_FF = int(math.ceil((8 * D_MODEL / 3) / 256) * 256)
EPS = 1e-6

MXU_COLS = 256
LANES = 128
SUBLANES = 8
DENOM_ROWS = 16
SHIFT_ROWS = 16
LOOP_BLOCKS = 4
MAX_FIXED_SHIFT = 40.0
VMEM_LIMIT_BYTES = 56 * 1024 * 1024

MASKED = -1e30
Q_SCALE = HEAD_DIM ** -0.5 * math.log2(math.e)


def _const_spec(shape):
    zeros = (0,) * len(shape)
    return pl.BlockSpec(shape, lambda *_: zeros, pipeline_mode=pl.Buffered(1))


def _params(n_grid_axes, flags=None):
    return pltpu.CompilerParams(
        dimension_semantics=("arbitrary",) * n_grid_axes,
        vmem_limit_bytes=VMEM_LIMIT_BYTES,
        flags=flags,
    )


def _modulated_norm(x, gain, shift, scale):
    y = x * lax.rsqrt(jnp.mean(x * x, axis=-1, keepdims=True) + EPS)
    return (y * gain) * (1.0 + scale) + shift


def _ff_chunks():
    chunks, start = [], 0
    while start < D_FF:
        size = min(2 * MXU_COLS, D_FF - start)
        chunks.append((start, size))
        start += size
    return chunks


def _swiglu(h_bf16, w_gu_ref, w_down_ref):
    acc = None
    for start, size in _ff_chunks():
        g = jnp.dot(h_bf16, w_gu_ref[:, start:start + size], preferred_element_type=jnp.float32)
        u = jnp.dot(h_bf16, w_gu_ref[:, D_FF + start:D_FF + start + size],
                    preferred_element_type=jnp.float32)
        act = (g * jax.nn.sigmoid(g) * u).astype(jnp.bfloat16)
        part = jnp.dot(act, w_down_ref[start:start + size, :], preferred_element_type=jnp.float32)
        acc = part if acc is None else acc + part
    return acc


def _adaln_kernel(c_ref, w_ref, b_ref, o_ref):
    c = c_ref[...]
    sc = (c * jax.nn.sigmoid(c)).astype(jnp.bfloat16)
    w = w_ref[0].astype(jnp.bfloat16)
    o_ref[0] = jnp.dot(sc, w, preferred_element_type=jnp.float32) + b_ref[0]


def _adaln_mod(c, w_ada, b_ada):
    depth, _, n_out = w_ada.shape
    batch = c.shape[0]
    n_chunk = n_out // 4
    return pl.pallas_call(
        _adaln_kernel,
        out_shape=jax.ShapeDtypeStruct((depth, batch, n_out), jnp.float32),
        grid=(depth, n_out // n_chunk),
        in_specs=[
            pl.BlockSpec((batch, D_MODEL), lambda i, j: (0, 0)),
            pl.BlockSpec((1, D_MODEL, n_chunk), lambda i, j: (i, 0, j)),
            pl.BlockSpec((1, 1, n_chunk), lambda i, j: (i, 0, j)),
        ],
        out_specs=pl.BlockSpec((1, batch, n_chunk), lambda i, j: (i, 0, j)),
        compiler_params=_params(2),
        name="adaln_mod",
    )(c, w_ada, b_ada.reshape(depth, 1, n_out))


def _attn_pre_kernel(x_ref, mod_ref, gain_ref, w_ref, qg_ref, kg_ref,
                     qt_ref, k_ref, vt_ref, kmean_ref, *, tm):
    mod = mod_ref[0, 0]
    h = _modulated_norm(x_ref[0], gain_ref[...], mod[0:1], mod[1:2]).astype(jnp.bfloat16)
    n_blk = tm // BLOCK
    heads_per_chunk = MXU_COLS // HEAD_DIM
    for chunk in range(3 * D_MODEL // MXU_COLS):
        r = jnp.dot(h, w_ref[:, chunk * MXU_COLS:(chunk + 1) * MXU_COLS],
                    preferred_element_type=jnp.float32)
        kind = (chunk * heads_per_chunk) // N_HEADS
        for half in range(heads_per_chunk):
            head = (chunk * heads_per_chunk) % N_HEADS + half
            t = r[:, half * HEAD_DIM:(half + 1) * HEAD_DIM]
            if kind == 2:
                for blk in range(n_blk):
                    vt_ref[0, head, blk] = t[blk * BLOCK:(blk + 1) * BLOCK].T.astype(jnp.bfloat16)
                continue
            tn = t * lax.rsqrt(jnp.mean(t * t, axis=-1, keepdims=True) + EPS)
            if kind == 0:
                qn = (tn * qg_ref[...]) * Q_SCALE
                for blk in range(n_blk):
                    qt_ref[0, head, blk] = qn[blk * BLOCK:(blk + 1) * BLOCK].T.astype(jnp.bfloat16)
            else:
                kn = tn * kg_ref[...]
                k_ref[0, head] = kn.astype(jnp.bfloat16)
                kmean_ref[0, 0, head] = jnp.mean(kn.reshape(n_blk, BLOCK, HEAD_DIM), axis=1)


def _attn_pre(x, mod, gain, w_qkv_bf16, q_gain, k_gain, *, tm):
    batch, seq, _ = x.shape
    n_tiles = seq // tm
    n_blk = tm // BLOCK
    t_shape = jax.ShapeDtypeStruct((batch, N_HEADS, seq // BLOCK, HEAD_DIM, BLOCK), jnp.bfloat16)
    t_spec = pl.BlockSpec((1, N_HEADS, n_blk, HEAD_DIM, BLOCK), lambda b, t: (b, 0, t, 0, 0))
    return pl.pallas_call(
        functools.partial(_attn_pre_kernel, tm=tm),
        out_shape=(
            t_shape,
            jax.ShapeDtypeStruct((batch, N_HEADS, seq, HEAD_DIM), jnp.bfloat16),
            t_shape,
            jax.ShapeDtypeStruct((batch, n_tiles, N_HEADS, n_blk, HEAD_DIM), jnp.float32),
        ),
        grid=(batch, n_tiles),
        in_specs=[
            pl.BlockSpec((1, tm, D_MODEL), lambda b, t: (b, t, 0)),
            pl.BlockSpec((1, 1, 6, D_MODEL), lambda b, t: (0, b, 0, 0)),
            _const_spec((1, D_MODEL)),
            _const_spec((D_MODEL, 3 * D_MODEL)),
            _const_spec((1, HEAD_DIM)),
            _const_spec((1, HEAD_DIM)),
        ],
        out_specs=(
            t_spec,
            pl.BlockSpec((1, N_HEADS, tm, HEAD_DIM), lambda b, t: (b, 0, t, 0)),
            t_spec,
            pl.BlockSpec((1, 1, N_HEADS, n_blk, HEAD_DIM), lambda b, t: (b, t, 0, 0, 0)),
        ),
        compiler_params=_params(2),
        name="attn_pre",
    )(x, mod, gain, w_qkv_bf16, q_gain, k_gain)


def _moba_kernel(qt_ref, k_ref, vt_ref, kmean_ref, shift_ref, o_ref,
                 q2t_ref, stage_ref, acc_ref, *stat_refs, n_blocks, fixed_shift):
    nb = pl.program_id(1)
    blk_id = lax.broadcasted_iota(jnp.int32, (n_blocks, BLOCK), 0).astype(jnp.float32)
    key_pos = lax.broadcasted_iota(jnp.int32, (BLOCK, BLOCK), 0)
    qry_pos = lax.broadcasted_iota(jnp.int32, (BLOCK, BLOCK), 1)
    lane_id = lax.broadcasted_iota(jnp.int32, (BLOCK, LANES), 1)
    neg_inf = jnp.float32(-jnp.inf)
    zero_rows = jnp.zeros((BLOCK - HEAD_DIM - n_blocks - SHIFT_ROWS, BLOCK), jnp.bfloat16)
    ones_rows = jnp.ones((DENOM_ROWS, BLOCK), jnp.bfloat16)
    if not fixed_shift:
        mb_ref, m_ref = stat_refs

    def route_scores(h):
        kmean = kmean_ref[0, h]
        km_hi = kmean.astype(jnp.bfloat16)
        km_lo = (kmean - km_hi.astype(jnp.float32)).astype(jnp.bfloat16)
        r2 = jnp.dot(jnp.concatenate([km_hi, km_lo], axis=0), qt_ref[0, h, 0],
                     preferred_element_type=jnp.float32)
        return r2[:n_blocks] + r2[n_blocks:]

    def route_mask(score):
        score = jnp.where(blk_id < nb.astype(jnp.float32), score, neg_inf)
        mask = jnp.full((n_blocks, BLOCK), MASKED, jnp.float32)
        for _ in range(TOPK_BLOCKS):
            best = jnp.max(score, axis=0, keepdims=True)
            first = jnp.min(jnp.where(score == best, blk_id, float(n_blocks)), axis=0, keepdims=True)
            pick = blk_id == first
            mask = jnp.where(pick & (best > neg_inf), 0.0, mask)
            score = jnp.where(pick, neg_inf, score)
        return mask

    def key_extension(j):
        hit = None if j is None else lane_id == j
        if fixed_shift:
            hit = lane_id == n_blocks if hit is None else hit | (lane_id == n_blocks)
        if hit is None:
            return jnp.zeros((BLOCK, LANES), jnp.bfloat16)
        return jnp.where(hit, 1.0, 0.0).astype(jnp.bfloat16)

    def produce(h, j, extension, slot, own):
        keys = jnp.concatenate([k_ref[0, h, j], extension], axis=1)
        s = jnp.dot(keys, q2t_ref[h], preferred_element_type=jnp.float32)
        if own:
            s = jnp.where(key_pos <= qry_pos, s, MASKED)
        if fixed_shift:
            stage_ref[slot, h] = jnp.exp2(s).astype(jnp.bfloat16)
        else:
            stage_ref[slot, h] = s
            mb_ref[slot, h] = jnp.max(s, axis=0, keepdims=True)

    def consume(h, j, slot, first):
        values = jnp.concatenate([vt_ref[0, h, j], ones_rows], axis=0)
        if fixed_shift:
            pv = jnp.dot(values, stage_ref[slot, h], preferred_element_type=jnp.float32)
            acc_ref[h] = pv if first else acc_ref[h] + pv
            return
        m_blk = mb_ref[slot, h]
        if first:
            m_new = m_blk
        else:
            m = m_ref[h]
            m_new = jnp.maximum(m, m_blk)
            alpha = jnp.exp2(m - m_new)
        p = jnp.exp2(stage_ref[slot, h] - m_new).astype(jnp.bfloat16)
        pv = jnp.dot(values, p, preferred_element_type=jnp.float32)
        m_ref[h] = m_new
        acc_ref[h] = pv if first else alpha * acc_ref[h] + pv

    def run_blocks(first_block, count):
        for t in range(count):
            j = first_block + t
            j_next = jnp.minimum(j + 1, n_blocks - 1)
            extension = key_extension(j_next)
            for h in range(N_HEADS):
                produce(h, j_next, extension, (t + 1) % 2, False)
                consume(h, j, t % 2, False)

    unrouted = jnp.concatenate([jnp.zeros((n_blocks, BLOCK), jnp.float32), shift_ref[...]],
                               axis=0).astype(jnp.bfloat16)
    for h in range(N_HEADS):
        q2t_ref[h] = jnp.concatenate([qt_ref[0, h, 0], unrouted, zero_rows], axis=0)
    routing_scores = [route_scores(h) for h in range(N_HEADS)]
    extension = key_extension(None)
    for h in range(N_HEADS):
        produce(h, nb, extension, 1, True)
    for h in range(N_HEADS):
        q2t_ref[h, HEAD_DIM:HEAD_DIM + n_blocks, :] = route_mask(routing_scores[h]).astype(jnp.bfloat16)
    extension = key_extension(0)
    for h in range(N_HEADS):
        produce(h, 0, extension, 0, False)
        consume(h, nb, 1, True)

    def block_group(i, _):
        run_blocks(LOOP_BLOCKS * i, LOOP_BLOCKS)
        return 0

    n_groups = lax.shift_right_logical(nb, LOOP_BLOCKS.bit_length() - 1)
    lax.fori_loop(0, n_groups, block_group, 0)
    done = n_groups * LOOP_BLOCKS
    count = LOOP_BLOCKS // 2
    while count:
        @pl.when(jnp.bitwise_and(nb, count) != 0)
        def _(first_block=done, count=count):
            run_blocks(first_block, count)
        done = done + jnp.bitwise_and(nb, count)
        count //= 2
    for h in range(N_HEADS):
        out_t = acc_ref[h, 0:HEAD_DIM, :] / acc_ref[h, HEAD_DIM:HEAD_DIM + 1, :]
        o_ref[0, :, h * HEAD_DIM:(h + 1) * HEAD_DIM] = out_t.T.astype(o_ref.dtype)


def _moba_attention(qt, k, vt, kmean, shift_rows, *, fixed_shift):
    batch, n_heads, n_blocks, _, _ = qt.shape
    seq = n_blocks * BLOCK
    k5 = k.reshape(batch, n_heads, n_blocks, BLOCK, HEAD_DIM)
    per_batch = lambda b, nb: (b,) + (0,) * 4
    stage_dtype = jnp.bfloat16 if fixed_shift else jnp.float32
    scratch = [
        pltpu.VMEM((n_heads, BLOCK, BLOCK), jnp.bfloat16),
        pltpu.VMEM((2, n_heads, BLOCK, BLOCK), stage_dtype),
        pltpu.VMEM((n_heads, HEAD_DIM + DENOM_ROWS, BLOCK), jnp.float32),
    ]
    if not fixed_shift:
        scratch += [
            pltpu.VMEM((2, n_heads, 1, BLOCK), jnp.float32),
            pltpu.VMEM((n_heads, 1, BLOCK), jnp.float32),
        ]
    return pl.pallas_call(
        functools.partial(_moba_kernel, n_blocks=n_blocks, fixed_shift=fixed_shift),
        out_shape=jax.ShapeDtypeStruct((batch, seq, D_MODEL), jnp.bfloat16),
        grid=(batch, n_blocks),
        in_specs=[
            pl.BlockSpec((1, n_heads, 1, HEAD_DIM, BLOCK), lambda b, nb: (b, 0, nb, 0, 0)),
            pl.BlockSpec((1, n_heads, n_blocks, BLOCK, HEAD_DIM), per_batch, pipeline_mode=pl.Buffered(1)),
            pl.BlockSpec((1, n_heads, n_blocks, HEAD_DIM, BLOCK), per_batch, pipeline_mode=pl.Buffered(1)),
            pl.BlockSpec((1, n_heads, n_blocks, HEAD_DIM), lambda b, nb: (b, 0, 0, 0)),
            _const_spec(shift_rows.shape),
        ],
        out_specs=pl.BlockSpec((1, BLOCK, D_MODEL), lambda b, nb: (b, nb, 0)),
        scratch_shapes=scratch,
        compiler_params=_params(2),
        name="moba_attn_fixed_shift" if fixed_shift else "moba_attn_online",
    )(qt, k5, vt, kmean, shift_rows)


def _attn_post_kernel(x_ref, o_ref, mod_ref, gain_ref, w_o_ref, w_gu_ref, w_down_ref, out_ref):
    mod = mod_ref[0, 0]
    y = jnp.dot(o_ref[0], w_o_ref[...], preferred_element_type=jnp.float32)
    x = x_ref[0] + mod[2:3] * y
    h = _modulated_norm(x, gain_ref[...], mod[3:4], mod[4:5]).astype(jnp.bfloat16)
    out_ref[0] = x + mod[5:6] * _swiglu(h, w_gu_ref, w_down_ref)


def _attn_post(x, o, mod, gain_ffn, w_o, w_gu, w_down, *, tm):
    batch, seq, _ = x.shape
    tile = lambda b, t: (b, t, 0)
    return pl.pallas_call(
        _attn_post_kernel,
        out_shape=jax.ShapeDtypeStruct(x.shape, x.dtype),
        grid=(batch, seq // tm),
        in_specs=[
            pl.BlockSpec((1, tm, D_MODEL), tile),
            pl.BlockSpec((1, tm, D_MODEL), tile),
            pl.BlockSpec((1, 1, 6, D_MODEL), lambda b, t: (0, b, 0, 0)),
            _const_spec((1, D_MODEL)),
            _const_spec(w_o.shape),
            _const_spec(w_gu.shape),
            _const_spec(w_down.shape),
        ],
        out_specs=pl.BlockSpec((1, tm, D_MODEL), tile),
        compiler_params=_params(2),
        name="attn_post",
    )(x, o, mod, gain_ffn, w_o, w_gu, w_down)


def _conv_layer_kernel(x_ref, mod_ref, gain_mix_ref, gain_ffn_ref, w_in_ref, conv_w_ref,
                       w_out_ref, w_gu_ref, w_down_ref, out_ref, u_ref, *, tm):
    mod = mod_ref[0, 0]
    x = x_ref[0]
    h = _modulated_norm(x, gain_mix_ref[...], mod[0:1], mod[1:2]).astype(jnp.bfloat16)

    @pl.when(pl.program_id(1) == 0)
    def _():
        u_ref[0:SUBLANES, :] = jnp.zeros((SUBLANES, D_MODEL), jnp.float32)

    @pl.when(pl.program_id(1) > 0)
    def _():
        u_ref[0:SUBLANES, :] = u_ref[tm:tm + SUBLANES, :]

    b_gate = jnp.dot(h, w_in_ref[:, :D_MODEL], preferred_element_type=jnp.float32)
    c_gate = jnp.dot(h, w_in_ref[:, D_MODEL:2 * D_MODEL], preferred_element_type=jnp.float32)
    u = c_gate * jnp.dot(h, w_in_ref[:, 2 * D_MODEL:], preferred_element_type=jnp.float32)
    u_ref[SUBLANES:, :] = u
    y = conv_w_ref[CONV_WIDTH - 1:CONV_WIDTH, :] * u
    for tap in range(CONV_WIDTH - 1):
        back = CONV_WIDTH - 1 - tap
        y = y + conv_w_ref[tap:tap + 1, :] * u_ref[SUBLANES - back:SUBLANES - back + tm, :]
    mix = jnp.dot((b_gate * y).astype(jnp.bfloat16), w_out_ref[...], preferred_element_type=jnp.float32)
    x = x + mod[2:3] * mix
    h = _modulated_norm(x, gain_ffn_ref[...], mod[3:4], mod[4:5]).astype(jnp.bfloat16)
    out_ref[0] = x + mod[5:6] * _swiglu(h, w_gu_ref, w_down_ref)


def _conv_layer(x, mod, gain_mix, gain_ffn, w_in, conv_w, w_out, w_gu, w_down, *, tm):
    batch, seq, _ = x.shape
    tile = lambda b, t: (b, t, 0)
    return pl.pallas_call(
        functools.partial(_conv_layer_kernel, tm=tm),
        out_shape=jax.ShapeDtypeStruct(x.shape, x.dtype),
        grid=(batch, seq // tm),
        in_specs=[
            pl.BlockSpec((1, tm, D_MODEL), tile),
            pl.BlockSpec((1, 1, 6, D_MODEL), lambda b, t: (1, b, 0, 0)),
            _const_spec((1, D_MODEL)),
            _const_spec((1, D_MODEL)),
            _const_spec(w_in.shape),
            _const_spec(conv_w.shape),
            _const_spec(w_out.shape),
            _const_spec(w_gu.shape),
            _const_spec(w_down.shape),
        ],
        out_specs=pl.BlockSpec((1, tm, D_MODEL), tile),
        scratch_shapes=[pltpu.VMEM((tm + SUBLANES, D_MODEL), jnp.float32)],
        compiler_params=_params(2),
        name="conv_layer",
    )(x, mod, gain_mix, gain_ffn, w_in, conv_w, w_out, w_gu, w_down)


def kernel(x, c, w_ada, b_ada, norm_mix, norm_ffn, w_qkv, w_o, q_gain, k_gain,
           w_in, conv_w, w_out, w_gate_up, w_down):
    batch, seq, _ = x.shape
    depth = w_ada.shape[0]
    assert depth == 2 and seq % BLOCK == 0
    bf16 = jnp.bfloat16
    mod = _adaln_mod(c, w_ada, b_ada).reshape(depth, batch, 6, D_MODEL)

    tm_pre, tm_ffn = 512, 512
    q, k, vt, kmean = _attn_pre(x, mod, norm_mix[0:1], w_qkv[0].astype(bf16), q_gain[0:1], k_gain[0:1],
                                tm=tm_pre)
    kmean = kmean.transpose(0, 2, 1, 3, 4).reshape(batch, N_HEADS, seq // BLOCK, HEAD_DIM)
    score_bound = HEAD_DIM * Q_SCALE * jnp.max(jnp.abs(q_gain[0])) * jnp.max(jnp.abs(k_gain[0]))
    shift_rows = jnp.zeros((SHIFT_ROWS, BLOCK), jnp.float32).at[0].set(-score_bound)
    o = lax.cond(
        score_bound <= MAX_FIXED_SHIFT,
        functools.partial(_moba_attention, fixed_shift=True),
        functools.partial(_moba_attention, fixed_shift=False),
        q, k, vt, kmean, shift_rows)
    x = _attn_post(x, o, mod, norm_ffn[0:1], w_o[0].astype(bf16), w_gate_up[0].astype(bf16),
                   w_down[0].astype(bf16), tm=tm_ffn)
    x = _conv_layer(x, mod, norm_mix[1:2], norm_ffn[1:2], w_in[0].astype(bf16), conv_w[0],
                    w_out[0].astype(bf16), w_gate_up[1].astype(bf16), w_down[1].astype(bf16), tm=tm_ffn)
    return x
```

```python
import functools
import math

import jax
import jax.numpy as jnp
from jax import lax
from jax.experimental import pallas as pl
from jax.experimental.pallas import tpu as pltpu

D_MODEL = 1024
N_HEADS = 8
HEAD_DIM = D_MODEL // N_HEADS
BLOCK = 256
TOPK_BLOCKS = 3
CONV_WIDTH = 3
D_FF = int(math.ceil((8 * D_MODEL / 3) / 256) * 256)
EPS = 1e-6

MXU_COLS = 256
LANES = 128
SUBLANES = 8
DENOM_ROWS = 16
SHIFT_ROWS = 16
LOOP_BLOCKS = 8
MAX_FIXED_SHIFT = 40.0
VMEM_LIMIT_BYTES = 56 * 1024 * 1024

MASKED = -1e30
Q_SCALE = HEAD_DIM ** -0.5 * math.log2(math.e)


def _const_spec(shape):
    zeros = (0,) * len(shape)
    return pl.BlockSpec(shape, lambda *_: zeros, pipeline_mode=pl.Buffered(1))


def _params(n_grid_axes, flags=None):
    return pltpu.CompilerParams(
        dimension_semantics=("arbitrary",) * n_grid_axes,
        vmem_limit_bytes=VMEM_LIMIT_BYTES,
        flags=flags,
    )


def _modulated_norm(x, gain, shift, scale):
    y = x * lax.rsqrt(jnp.mean(x * x, axis=-1, keepdims=True) + EPS)
    return y * (gain * (1.0 + scale)) + shift


def _ff_chunks():
    chunks, start = [], 0
    while start < D_FF:
        size = min(2 * MXU_COLS, D_FF - start)
        chunks.append((start, size))
        start += size
    return chunks


def _swiglu(h_bf16, w_gu_ref, w_down_ref):
    acc = None
    for start, size in _ff_chunks():
        g = jnp.dot(h_bf16, w_gu_ref[:, start:start + size], preferred_element_type=jnp.float32)
        u = jnp.dot(h_bf16, w_gu_ref[:, D_FF + start:D_FF + start + size],
                    preferred_element_type=jnp.float32)
        act = (g * jax.nn.sigmoid(g) * u).astype(jnp.bfloat16)
        part = jnp.dot(act, w_down_ref[start:start + size, :], preferred_element_type=jnp.float32)
        acc = part if acc is None else acc + part
    return acc


def _adaln_kernel(c_ref, w_ref, b_ref, o_ref):
    c = c_ref[...]
    sc = (c * jax.nn.sigmoid(c)).astype(jnp.bfloat16)
    w = w_ref[0].astype(jnp.bfloat16)
    o_ref[0] = jnp.dot(sc, w, preferred_element_type=jnp.float32) + b_ref[0]


def _adaln_mod(c, w_ada, b_ada):
    depth, _, n_out = w_ada.shape
    batch = c.shape[0]
    n_chunk = n_out // 4
    return pl.pallas_call(
        _adaln_kernel,
        out_shape=jax.ShapeDtypeStruct((depth, batch, n_out), jnp.float32),
        grid=(depth, n_out // n_chunk),
        in_specs=[
            pl.BlockSpec((batch, D_MODEL), lambda i, j: (0, 0)),
            pl.BlockSpec((1, D_MODEL, n_chunk), lambda i, j: (i, 0, j)),
            pl.BlockSpec((1, 1, n_chunk), lambda i, j: (i, 0, j)),
        ],
        out_specs=pl.BlockSpec((1, batch, n_chunk), lambda i, j: (i, 0, j)),
        compiler_params=_params(2),
        name="adaln_mod",
    )(c, w_ada, b_ada.reshape(depth, 1, n_out))


def _attn_pre_kernel(x_ref, mod_ref, gain_ref, w_ref, qg_ref, kg_ref,
                     qt_ref, k_ref, vt_ref, kmean_ref, *, tm):
    mod = mod_ref[0, 0]
    h = _modulated_norm(x_ref[0], gain_ref[...], mod[0:1], mod[1:2]).astype(jnp.bfloat16)
    n_blk = tm // BLOCK
    heads_per_chunk = MXU_COLS // HEAD_DIM
    for chunk in range(3 * D_MODEL // MXU_COLS):
        r = jnp.dot(h, w_ref[:, chunk * MXU_COLS:(chunk + 1) * MXU_COLS],
                    preferred_element_type=jnp.float32)
        kind = (chunk * heads_per_chunk) // N_HEADS
        for half in range(heads_per_chunk):
            head = (chunk * heads_per_chunk) % N_HEADS + half
            t = r[:, half * HEAD_DIM:(half + 1) * HEAD_DIM]
            if kind == 2:
                for blk in range(n_blk):
                    vt_ref[0, head, blk] = t[blk * BLOCK:(blk + 1) * BLOCK].T.astype(jnp.bfloat16)
                continue
            tn = t * lax.rsqrt(jnp.mean(t * t, axis=-1, keepdims=True) + EPS)
            if kind == 0:
                qn = (tn * qg_ref[...]) * Q_SCALE
                for blk in range(n_blk):
                    qt_ref[0, head, blk] = qn[blk * BLOCK:(blk + 1) * BLOCK].T.astype(jnp.bfloat16)
            else:
                kn = tn * kg_ref[...]
                k_ref[0, head] = kn.astype(jnp.bfloat16)
                kmean_ref[0, 0, head] = jnp.mean(kn.reshape(n_blk, BLOCK, HEAD_DIM), axis=1)


def _attn_pre(x, mod, gain, w_qkv_bf16, q_gain, k_gain, *, tm):
    batch, seq, _ = x.shape
    n_tiles = seq // tm
    n_blk = tm // BLOCK
    t_shape = jax.ShapeDtypeStruct((batch, N_HEADS, seq // BLOCK, HEAD_DIM, BLOCK), jnp.bfloat16)
    t_spec = pl.BlockSpec((1, N_HEADS, n_blk, HEAD_DIM, BLOCK), lambda b, t: (b, 0, t, 0, 0))
    return pl.pallas_call(
        functools.partial(_attn_pre_kernel, tm=tm),
        out_shape=(
            t_shape,
            jax.ShapeDtypeStruct((batch, N_HEADS, seq, HEAD_DIM), jnp.bfloat16),
            t_shape,
            jax.ShapeDtypeStruct((batch, n_tiles, N_HEADS, n_blk, HEAD_DIM), jnp.float32),
        ),
        grid=(batch, n_tiles),
        in_specs=[
            pl.BlockSpec((1, tm, D_MODEL), lambda b, t: (b, t, 0)),
            pl.BlockSpec((1, 1, 6, D_MODEL), lambda b, t: (0, b, 0, 0)),
            _const_spec((1, D_MODEL)),
            _const_spec((D_MODEL, 3 * D_MODEL)),
            _const_spec((1, HEAD_DIM)),
            _const_spec((1, HEAD_DIM)),
        ],
        out_specs=(
            t_spec,
            pl.BlockSpec((1, N_HEADS, tm, HEAD_DIM), lambda b, t: (b, 0, t, 0)),
            t_spec,
            pl.BlockSpec((1, 1, N_HEADS, n_blk, HEAD_DIM), lambda b, t: (b, t, 0, 0, 0)),
        ),
        compiler_params=_params(2),
        name="attn_pre",
    )(x, mod, gain, w_qkv_bf16, q_gain, k_gain)


def _moba_kernel(qt_ref, k_ref, vt_ref, kmean_ref, shift_ref, o_ref,
                 q2t_ref, stage_ref, acc_ref, *stat_refs, n_blocks, fixed_shift):
    nb = pl.program_id(1)
    blk_id = lax.broadcasted_iota(jnp.int32, (n_blocks, BLOCK), 0).astype(jnp.float32)
    key_pos = lax.broadcasted_iota(jnp.int32, (BLOCK, BLOCK), 0)
    qry_pos = lax.broadcasted_iota(jnp.int32, (BLOCK, BLOCK), 1)
    lane_id = lax.broadcasted_iota(jnp.int32, (BLOCK, LANES), 1)
    neg_inf = jnp.float32(-jnp.inf)
    zero_rows = jnp.zeros((BLOCK - HEAD_DIM - n_blocks - SHIFT_ROWS, BLOCK), jnp.bfloat16)
    ones_rows = jnp.ones((DENOM_ROWS, BLOCK), jnp.bfloat16)
    if not fixed_shift:
        mb_ref, m_ref = stat_refs

    def route_scores(h):
        kmean = kmean_ref[0, h]
        km_hi = kmean.astype(jnp.bfloat16)
        km_lo = (kmean - km_hi.astype(jnp.float32)).astype(jnp.bfloat16)
        r2 = jnp.dot(jnp.concatenate([km_hi, km_lo], axis=0), qt_ref[0, h, 0],
                     preferred_element_type=jnp.float32)
        return r2[:n_blocks] + r2[n_blocks:]

    def route_mask(score):
        score = jnp.where(blk_id < nb.astype(jnp.float32), score, neg_inf)
        mask = jnp.full((n_blocks, BLOCK), MASKED, jnp.float32)
        for _ in range(TOPK_BLOCKS):
            best = jnp.max(score, axis=0, keepdims=True)
            first = jnp.min(jnp.where(score == best, blk_id, float(n_blocks)), axis=0, keepdims=True)
            pick = blk_id == first
            mask = jnp.where(pick & (best > neg_inf), 0.0, mask)
            score = jnp.where(pick, neg_inf, score)
        return mask

    def key_extension(j):
        hit = None if j is None else lane_id == j
        if fixed_shift:
            hit = lane_id == n_blocks if hit is None else hit | (lane_id == n_blocks)
        if hit is None:
            return jnp.zeros((BLOCK, LANES), jnp.bfloat16)
        return jnp.where(hit, 1.0, 0.0).astype(jnp.bfloat16)

    def produce(h, j, extension, slot, own):
        keys = jnp.concatenate([k_ref[0, h, j], extension], axis=1)
        s = jnp.dot(keys, q2t_ref[h], preferred_element_type=jnp.float32)
        if own:
            s = jnp.where(key_pos <= qry_pos, s, MASKED)
        if fixed_shift:
            stage_ref[slot, h] = jnp.exp2(s).astype(jnp.bfloat16)
        else:
            stage_ref[slot, h] = s
            mb_ref[slot, h] = jnp.max(s, axis=0, keepdims=True)

    def consume(h, j, slot, first):
        values = jnp.concatenate([vt_ref[0, h, j], ones_rows], axis=0)
        if fixed_shift:
            pv = jnp.dot(values, stage_ref[slot, h], preferred_element_type=jnp.float32)
            acc_ref[h] = pv if first else acc_ref[h] + pv
            return
        m_blk = mb_ref[slot, h]
        if first:
            m_new = m_blk
        else:
            m = m_ref[h]
            m_new = jnp.maximum(m, m_blk)
            alpha = jnp.exp2(m - m_new)
        p = jnp.exp2(stage_ref[slot, h] - m_new).astype(jnp.bfloat16)
        pv = jnp.dot(values, p, preferred_element_type=jnp.float32)
        m_ref[h] = m_new
        acc_ref[h] = pv if first else alpha * acc_ref[h] + pv

    def run_blocks(first_block, count):
        for t in range(count):
            j = first_block + t
            j_next = jnp.minimum(j + 1, n_blocks - 1)
            extension = key_extension(j_next)
            for h in range(N_HEADS):
                produce(h, j_next, extension, (t + 1) % 2, False)
                consume(h, j, t % 2, False)

    unrouted = jnp.concatenate([jnp.zeros((n_blocks, BLOCK), jnp.float32), shift_ref[...]],
                               axis=0).astype(jnp.bfloat16)
    for h in range(N_HEADS):
        q2t_ref[h] = jnp.concatenate([qt_ref[0, h, 0], unrouted, zero_rows], axis=0)
    routing_scores = [route_scores(h) for h in range(N_HEADS)]
    extension = key_extension(None)
    for h in range(N_HEADS):
        produce(h, nb, extension, 1, True)
    for h in range(N_HEADS):
        q2t_ref[h, HEAD_DIM:HEAD_DIM + n_blocks, :] = route_mask(routing_scores[h]).astype(jnp.bfloat16)
    extension = key_extension(0)
    for h in range(N_HEADS):
        produce(h, 0, extension, 0, False)
        consume(h, nb, 1, True)

    def block_group(i, _):
        run_blocks(LOOP_BLOCKS * i, LOOP_BLOCKS)
        return 0

    n_groups = lax.shift_right_logical(nb, LOOP_BLOCKS.bit_length() - 1)
    lax.fori_loop(0, n_groups, block_group, 0)
    done = n_groups * LOOP_BLOCKS
    count = LOOP_BLOCKS // 2
    while count:
        @pl.when(jnp.bitwise_and(nb, count) != 0)
        def _(first_block=done, count=count):
            run_blocks(first_block, count)
        done = done + jnp.bitwise_and(nb, count)
        count //= 2
    for h in range(N_HEADS):
        out_t = acc_ref[h, 0:HEAD_DIM, :] / acc_ref[h, HEAD_DIM:HEAD_DIM + 1, :]
        o_ref[0, :, h * HEAD_DIM:(h + 1) * HEAD_DIM] = out_t.T.astype(o_ref.dtype)


def _moba_attention(qt, k, vt, kmean, shift_rows, *, fixed_shift):
    batch, n_heads, n_blocks, _, _ = qt.shape
    seq = n_blocks * BLOCK
    k5 = k.reshape(batch, n_heads, n_blocks, BLOCK, HEAD_DIM)
    per_batch = lambda b, nb: (b,) + (0,) * 4
    stage_dtype = jnp.bfloat16 if fixed_shift else jnp.float32
    scratch = [
        pltpu.VMEM((n_heads, BLOCK, BLOCK), jnp.bfloat16),
        pltpu.VMEM((2, n_heads, BLOCK, BLOCK), stage_dtype),
        pltpu.VMEM((n_heads, HEAD_DIM + DENOM_ROWS, BLOCK), jnp.float32),
    ]
    if not fixed_shift:
        scratch += [
            pltpu.VMEM((2, n_heads, 1, BLOCK), jnp.float32),
            pltpu.VMEM((n_heads, 1, BLOCK), jnp.float32),
        ]
    return pl.pallas_call(
        functools.partial(_moba_kernel, n_blocks=n_blocks, fixed_shift=fixed_shift),
        out_shape=jax.ShapeDtypeStruct((batch, seq, D_MODEL), jnp.bfloat16),
        grid=(batch, n_blocks),
        in_specs=[
            pl.BlockSpec((1, n_heads, 1, HEAD_DIM, BLOCK), lambda b, nb: (b, 0, nb, 0, 0)),
            pl.BlockSpec((1, n_heads, n_blocks, BLOCK, HEAD_DIM), per_batch, pipeline_mode=pl.Buffered(1)),
            pl.BlockSpec((1, n_heads, n_blocks, HEAD_DIM, BLOCK), per_batch, pipeline_mode=pl.Buffered(1)),
            pl.BlockSpec((1, n_heads, n_blocks, HEAD_DIM), lambda b, nb: (b, 0, 0, 0)),
            _const_spec(shift_rows.shape),
        ],
        out_specs=pl.BlockSpec((1, BLOCK, D_MODEL), lambda b, nb: (b, nb, 0)),
        scratch_shapes=scratch,
        compiler_params=_params(2),
        name="moba_attn_fixed_shift" if fixed_shift else "moba_attn_online",
    )(qt, k5, vt, kmean, shift_rows)


def _attn_post_kernel(x_ref, o_ref, mod_ref, gain_ref, w_o_ref, w_gu_ref, w_down_ref, out_ref):
    mod = mod_ref[0, 0]
    y = jnp.dot(o_ref[0], w_o_ref[...], preferred_element_type=jnp.float32)
    x = x_ref[0] + mod[2:3] * y
    h = _modulated_norm(x, gain_ref[...], mod[3:4], mod[4:5]).astype(jnp.bfloat16)
    out_ref[0] = x + mod[5:6] * _swiglu(h, w_gu_ref, w_down_ref)


def _attn_post(x, o, mod, gain_ffn, w_o, w_gu, w_down, *, tm):
    batch, seq, _ = x.shape
    tile = lambda b, t: (b, t, 0)
    return pl.pallas_call(
        _attn_post_kernel,
        out_shape=jax.ShapeDtypeStruct(x.shape, x.dtype),
        grid=(batch, seq // tm),
        in_specs=[
            pl.BlockSpec((1, tm, D_MODEL), tile),
            pl.BlockSpec((1, tm, D_MODEL), tile),
            pl.BlockSpec((1, 1, 6, D_MODEL), lambda b, t: (0, b, 0, 0)),
            _const_spec((1, D_MODEL)),
            _const_spec(w_o.shape),
            _const_spec(w_gu.shape),
            _const_spec(w_down.shape),
        ],
        out_specs=pl.BlockSpec((1, tm, D_MODEL), tile),
        compiler_params=_params(2),
        name="attn_post",
    )(x, o, mod, gain_ffn, w_o, w_gu, w_down)


def _conv_layer_kernel(x_ref, mod_ref, gain_mix_ref, gain_ffn_ref, w_in_ref, conv_w_ref,
                       w_out_ref, w_gu_ref, w_down_ref, out_ref, u_ref, *, tm):
    mod = mod_ref[0, 0]
    h = _modulated_norm(x_ref[0], gain_mix_ref[...], mod[0:1], mod[1:2]).astype(jnp.bfloat16)

    @pl.when(pl.program_id(1) == 0)
    def _():
        u_ref[0:SUBLANES, :] = jnp.zeros((SUBLANES, D_MODEL), jnp.float32)

    @pl.when(pl.program_id(1) > 0)
    def _():
        u_ref[0:SUBLANES, :] = u_ref[tm:tm + SUBLANES, :]

    b_gate = jnp.dot(h, w_in_ref[:, :D_MODEL], preferred_element_type=jnp.float32)
    c_gate = jnp.dot(h, w_in_ref[:, D_MODEL:2 * D_MODEL], preferred_element_type=jnp.float32)
    u = c_gate * jnp.dot(h, w_in_ref[:, 2 * D_MODEL:], preferred_element_type=jnp.float32)
    u_ref[SUBLANES:, :] = u
    y = conv_w_ref[CONV_WIDTH - 1:CONV_WIDTH, :] * u
    for tap in range(CONV_WIDTH - 1):
        back = CONV_WIDTH - 1 - tap
        y = y + conv_w_ref[tap:tap + 1, :] * u_ref[SUBLANES - back:SUBLANES - back + tm, :]
    mix = jnp.dot((b_gate * y).astype(jnp.bfloat16), w_out_ref[...], preferred_element_type=jnp.float32)
    x = x_ref[0] + mod[2:3] * mix
    h = _modulated_norm(x, gain_ffn_ref[...], mod[3:4], mod[4:5]).astype(jnp.bfloat16)
    out_ref[0] = x + mod[5:6] * _swiglu(h, w_gu_ref, w_down_ref)


def _conv_layer(x, mod, gain_mix, gain_ffn, w_in, conv_w, w_out, w_gu, w_down, *, tm):
    batch, seq, _ = x.shape
    tile = lambda b, t: (b, t, 0)
    return pl.pallas_call(
        functools.partial(_conv_layer_kernel, tm=tm),
        out_shape=jax.ShapeDtypeStruct(x.shape, x.dtype),
        grid=(batch, seq // tm),
        in_specs=[
            pl.BlockSpec((1, tm, D_MODEL), tile),
            pl.BlockSpec((1, 1, 6, D_MODEL), lambda b, t: (1, b, 0, 0)),
            _const_spec((1, D_MODEL)),
            _const_spec((1, D_MODEL)),
            _const_spec(w_in.shape),
            _const_spec(conv_w.shape),
            _const_spec(w_out.shape),
            _const_spec(w_gu.shape),
            _const_spec(w_down.shape),
        ],
        out_specs=pl.BlockSpec((1, tm, D_MODEL), tile),
        scratch_shapes=[pltpu.VMEM((tm + SUBLANES, D_MODEL), jnp.float32)],
        compiler_params=_params(2),
        name="conv_layer",
    )(x, mod, gain_mix, gain_ffn, w_in, conv_w, w_out, w_gu, w_down)


def kernel(x, c, w_ada, b_ada, norm_mix, norm_ffn, w_qkv, w_o, q_gain, k_gain,
           w_in, conv_w, w_out, w_gate_up, w_down):
    batch, seq, _ = x.shape
    depth = w_ada.shape[0]
    assert depth == 2 and seq % BLOCK == 0
    bf16 = jnp.bfloat16
    mod = _adaln_mod(c, w_ada, b_ada).reshape(depth, batch, 6, D_MODEL)

    tm_pre, tm_ffn = 512, 512
    q, k, vt, kmean = _attn_pre(x, mod, norm_mix[0:1], w_qkv[0].astype(bf16), q_gain[0:1], k_gain[0:1],
                                tm=tm_pre)
    kmean = kmean.transpose(0, 2, 1, 3, 4).reshape(batch, N_HEADS, seq // BLOCK, HEAD_DIM)
    score_bound = HEAD_DIM * Q_SCALE * jnp.max(jnp.abs(q_gain[0])) * jnp.max(jnp.abs(k_gain[0]))
    shift_rows = jnp.zeros((SHIFT_ROWS, BLOCK), jnp.float32).at[0].set(-score_bound)
    o = lax.cond(
        score_bound <= MAX_FIXED_SHIFT,
        functools.partial(_moba_attention, fixed_shift=True),
        functools.partial(_moba_attention, fixed_shift=False),
        q, k, vt, kmean, shift_rows)
    x = _attn_post(x, o, mod, norm_ffn[0:1], w_o[0].astype(bf16), w_gate_up[0].astype(bf16),
                   w_down[0].astype(bf16), tm=tm_ffn)
    x = _conv_layer(x, mod, norm_mix[1:2], norm_ffn[1:2], w_in[0].astype(bf16), conv_w[0],
                    w_out[0].astype(bf16), w_gate_up[1].astype(bf16), w_down[1].astype(bf16), tm=tm_ffn)
    return x
```

```python
import functools
import math

import jax
import jax.numpy as jnp
from jax import lax
from jax.experimental import pallas as pl
from jax.experimental.pallas import tpu as pltpu

D_MODEL = 1024
N_HEADS = 8
HEAD_DIM = D_MODEL // N_HEADS
BLOCK = 256
TOPK_BLOCKS = 3
CONV_WIDTH = 3
D_FF = int(math.ceil((8 * D_MODEL / 3) / 256) * 256)
EPS = 1e-6

MXU_COLS = 256
LANES = 128
SUBLANES = 8
DENOM_ROWS = 16
SHIFT_ROWS = 16
LOOP_BLOCKS = 8
MAX_FIXED_SHIFT = 40.0
VMEM_LIMIT_BYTES = 56 * 1024 * 1024

MASKED = -1e30
Q_SCALE = HEAD_DIM ** -0.5 * math.log2(math.e)


def _const_spec(shape):
    zeros = (0,) * len(shape)
    return pl.BlockSpec(shape, lambda *_: zeros, pipeline_mode=pl.Buffered(1))


def _layer_spec(stacked_shape, layer):
    tail = (0,) * (len(stacked_shape) - 1)
    return pl.BlockSpec((1,) + tuple(stacked_shape[1:]), lambda *_: (layer,) + tail,
                        pipeline_mode=pl.Buffered(1))


def _params(n_grid_axes, flags=None):
    return pltpu.CompilerParams(
        dimension_semantics=("arbitrary",) * n_grid_axes,
        vmem_limit_bytes=VMEM_LIMIT_BYTES,
        flags=flags,
    )


def _modulated_norm(x, gain, shift, scale):
    y = x * lax.rsqrt(jnp.mean(x * x, axis=-1, keepdims=True) + EPS)
    return y * (gain * (1.0 + scale)) + shift


def _ff_chunks():
    chunks, start = [], 0
    while start < D_FF:
        size = min(2 * MXU_COLS, D_FF - start)
        chunks.append((start, size))
        start += size
    return chunks


def _swiglu(h_bf16, w_gu_ref, w_down_ref):
    acc = None
    for start, size in _ff_chunks():
        g = jnp.dot(h_bf16, w_gu_ref[0, :, start:start + size], preferred_element_type=jnp.float32)
        u = jnp.dot(h_bf16, w_gu_ref[0, :, D_FF + start:D_FF + start + size],
                    preferred_element_type=jnp.float32)
        act = (g * jax.nn.sigmoid(g) * u).astype(jnp.bfloat16)
        part = jnp.dot(act, w_down_ref[0, start:start + size, :], preferred_element_type=jnp.float32)
        acc = part if acc is None else acc + part
    return acc


def _adaln_kernel(c_ref, w_ref, b_ref, o_ref):
    c = c_ref[...]
    sc = (c * jax.nn.sigmoid(c)).astype(jnp.bfloat16)
    w = w_ref[0].astype(jnp.bfloat16)
    o_ref[0] = jnp.dot(sc, w, preferred_element_type=jnp.float32) + b_ref[0]


def _adaln_mod(c, w_ada, b_ada):
    depth, _, n_out = w_ada.shape
    batch = c.shape[0]
    n_chunk = n_out // 4
    return pl.pallas_call(
        _adaln_kernel,
        out_shape=jax.ShapeDtypeStruct((depth, batch, n_out), jnp.float32),
        grid=(depth, n_out // n_chunk),
        in_specs=[
            pl.BlockSpec((batch, D_MODEL), lambda i, j: (0, 0)),
            pl.BlockSpec((1, D_MODEL, n_chunk), lambda i, j: (i, 0, j)),
            pl.BlockSpec((1, 1, n_chunk), lambda i, j: (i, 0, j)),
        ],
        out_specs=pl.BlockSpec((1, batch, n_chunk), lambda i, j: (i, 0, j)),
        compiler_params=_params(2),
        name="adaln_mod",
    )(c, w_ada, b_ada.reshape(depth, 1, n_out))


def _attn_pre_kernel(x_ref, mod_ref, gain_ref, w_ref, qg_ref, kg_ref,
                     qt_ref, k_ref, vt_ref, kmean_ref, *, tm):
    mod = mod_ref[0, 0]
    h = _modulated_norm(x_ref[0], gain_ref[...], mod[0:1], mod[1:2]).astype(jnp.bfloat16)
    n_blk = tm // BLOCK
    heads_per_chunk = MXU_COLS // HEAD_DIM
    for chunk in range(3 * D_MODEL // MXU_COLS):
        r = jnp.dot(h, w_ref[:, chunk * MXU_COLS:(chunk + 1) * MXU_COLS],
                    preferred_element_type=jnp.float32)
        kind = (chunk * heads_per_chunk) // N_HEADS
        for half in range(heads_per_chunk):
            head = (chunk * heads_per_chunk) % N_HEADS + half
            t = r[:, half * HEAD_DIM:(half + 1) * HEAD_DIM]
            if kind == 2:
                for blk in range(n_blk):
                    vt_ref[0, head, blk] = t[blk * BLOCK:(blk + 1) * BLOCK].T.astype(jnp.bfloat16)
                continue
            tn = t * lax.rsqrt(jnp.mean(t * t, axis=-1, keepdims=True) + EPS)
            if kind == 0:
                qn = (tn * qg_ref[...]) * Q_SCALE
                for blk in range(n_blk):
                    qt_ref[0, head, blk] = qn[blk * BLOCK:(blk + 1) * BLOCK].T.astype(jnp.bfloat16)
            else:
                kn = tn * kg_ref[...]
                k_ref[0, head] = kn.astype(jnp.bfloat16)
                kmean_ref[0, 0, head] = jnp.mean(kn.reshape(n_blk, BLOCK, HEAD_DIM), axis=1)


def _attn_pre(x, mod, gain, w_qkv_bf16, q_gain, k_gain, *, tm):
    batch, seq, _ = x.shape
    n_tiles = seq // tm
    n_blk = tm // BLOCK
    t_shape = jax.ShapeDtypeStruct((batch, N_HEADS, seq // BLOCK, HEAD_DIM, BLOCK), jnp.bfloat16)
    t_spec = pl.BlockSpec((1, N_HEADS, n_blk, HEAD_DIM, BLOCK), lambda b, t: (b, 0, t, 0, 0))
    return pl.pallas_call(
        functools.partial(_attn_pre_kernel, tm=tm),
        out_shape=(
            t_shape,
            jax.ShapeDtypeStruct((batch, N_HEADS, seq, HEAD_DIM), jnp.bfloat16),
            t_shape,
            jax.ShapeDtypeStruct((batch, n_tiles, N_HEADS, n_blk, HEAD_DIM), jnp.float32),
        ),
        grid=(batch, n_tiles),
        in_specs=[
            pl.BlockSpec((1, tm, D_MODEL), lambda b, t: (b, t, 0)),
            pl.BlockSpec((1, 1, 6, D_MODEL), lambda b, t: (0, b, 0, 0)),
            _const_spec((1, D_MODEL)),
            _const_spec((D_MODEL, 3 * D_MODEL)),
            _const_spec((1, HEAD_DIM)),
            _const_spec((1, HEAD_DIM)),
        ],
        out_specs=(
            t_spec,
            pl.BlockSpec((1, N_HEADS, tm, HEAD_DIM), lambda b, t: (b, 0, t, 0)),
            t_spec,
            pl.BlockSpec((1, 1, N_HEADS, n_blk, HEAD_DIM), lambda b, t: (b, t, 0, 0, 0)),
        ),
        compiler_params=_params(2),
        name="attn_pre",
    )(x, mod, gain, w_qkv_bf16, q_gain, k_gain)


def _moba_kernel(qt_ref, k_blk_ref, vt_blk_ref, kmean_ref, shift_ref, o_ref,
                 k_ref, vt_ref, q2t_ref, stage_ref, acc_ref, *stat_refs, n_blocks, fixed_shift):
    nb = pl.program_id(1)
    for h in range(N_HEADS):
        k_ref[h, nb] = k_blk_ref[0, h, 0]
        vt_ref[h, nb] = vt_blk_ref[0, h, 0]
    blk_id = lax.broadcasted_iota(jnp.int32, (n_blocks, BLOCK), 0).astype(jnp.float32)
    key_pos = lax.broadcasted_iota(jnp.int32, (BLOCK, BLOCK), 0)
    qry_pos = lax.broadcasted_iota(jnp.int32, (BLOCK, BLOCK), 1)
    lane_id = lax.broadcasted_iota(jnp.int32, (BLOCK, LANES), 1)
    neg_inf = jnp.float32(-jnp.inf)
    zero_rows = jnp.zeros((BLOCK - HEAD_DIM - n_blocks - SHIFT_ROWS, BLOCK), jnp.bfloat16)
    ones_rows = jnp.ones((DENOM_ROWS, BLOCK), jnp.bfloat16)
    if not fixed_shift:
        mb_ref, m_ref = stat_refs

    def route_scores(h):
        kmean = kmean_ref[0, h]
        km_hi = kmean.astype(jnp.bfloat16)
        km_lo = (kmean - km_hi.astype(jnp.float32)).astype(jnp.bfloat16)
        r2 = jnp.dot(jnp.concatenate([km_hi, km_lo], axis=0), qt_ref[0, h, 0],
                     preferred_element_type=jnp.float32)
        return r2[:n_blocks] + r2[n_blocks:]

    def route_mask(score):
        score = jnp.where(blk_id < nb.astype(jnp.float32), score, neg_inf)
        mask = jnp.full((n_blocks, BLOCK), MASKED, jnp.float32)
        for _ in range(TOPK_BLOCKS):
            best = jnp.max(score, axis=0, keepdims=True)
            first = jnp.min(jnp.where(score == best, blk_id, float(n_blocks)), axis=0, keepdims=True)
            pick = blk_id == first
            mask = jnp.where(pick & (best > neg_inf), 0.0, mask)
            score = jnp.where(pick, neg_inf, score)
        return mask

    def key_extension(j):
        hit = None if j is None else lane_id == j
        if fixed_shift:
            hit = lane_id == n_blocks if hit is None else hit | (lane_id == n_blocks)
        if hit is None:
            return jnp.zeros((BLOCK, LANES), jnp.bfloat16)
        return jnp.where(hit, 1.0, 0.0).astype(jnp.bfloat16)

    def produce(h, j, extension, slot, own):
        keys = jnp.concatenate([k_ref[h, j], extension], axis=1)
        s = jnp.dot(keys, q2t_ref[h], preferred_element_type=jnp.float32)
        if own:
            s = jnp.where(key_pos <= qry_pos, s, MASKED)
        if fixed_shift:
            stage_ref[slot, h] = jnp.exp2(s).astype(jnp.bfloat16)
        else:
            stage_ref[slot, h] = s
            mb_ref[slot, h] = jnp.max(s, axis=0, keepdims=True)

    def consume(h, j, slot, first):
        values = jnp.concatenate([vt_ref[h, j], ones_rows], axis=0)
        if fixed_shift:
            pv = jnp.dot(values, stage_ref[slot, h], preferred_element_type=jnp.float32)
            acc_ref[h] = pv if first else acc_ref[h] + pv
            return
        m_blk = mb_ref[slot, h]
        if first:
            m_new = m_blk
        else:
            m = m_ref[h]
            m_new = jnp.maximum(m, m_blk)
            alpha = jnp.exp2(m - m_new)
        p = jnp.exp2(stage_ref[slot, h] - m_new).astype(jnp.bfloat16)
        pv = jnp.dot(values, p, preferred_element_type=jnp.float32)
        m_ref[h] = m_new
        acc_ref[h] = pv if first else alpha * acc_ref[h] + pv

    def run_blocks(first_block, count):
        for t in range(count):
            j = first_block + t
            j_next = jnp.minimum(j + 1, n_blocks - 1)
            extension = key_extension(j_next)
            for h in range(N_HEADS):
                produce(h, j_next, extension, (t + 1) % 2, False)
                consume(h, j, t % 2, False)

    unrouted = jnp.concatenate([jnp.zeros((n_blocks, BLOCK), jnp.float32), shift_ref[...]],
                               axis=0).astype(jnp.bfloat16)
    for h in range(N_HEADS):
        q2t_ref[h] = jnp.concatenate([qt_ref[0, h, 0], unrouted, zero_rows], axis=0)
    routing_scores = [route_scores(h) for h in range(N_HEADS)]
    extension = key_extension(None)
    for h in range(N_HEADS):
        produce(h, nb, extension, 1, True)
    for h in range(N_HEADS):
        q2t_ref[h, HEAD_DIM:HEAD_DIM + n_blocks, :] = route_mask(routing_scores[h]).astype(jnp.bfloat16)
    extension = key_extension(0)
    for h in range(N_HEADS):
        produce(h, 0, extension, 0, False)
        consume(h, nb, 1, True)

    def block_group(i, _):
        run_blocks(LOOP_BLOCKS * i, LOOP_BLOCKS)
        return 0

    n_groups = lax.shift_right_logical(nb, LOOP_BLOCKS.bit_length() - 1)
    lax.fori_loop(0, n_groups, block_group, 0)
    done = n_groups * LOOP_BLOCKS
    count = LOOP_BLOCKS // 2
    while count:
        @pl.when(jnp.bitwise_and(nb, count) != 0)
        def _(first_block=done, count=count):
            run_blocks(first_block, count)
        done = done + jnp.bitwise_and(nb, count)
        count //= 2
    for h in range(N_HEADS):
        out_t = acc_ref[h, 0:HEAD_DIM, :] / acc_ref[h, HEAD_DIM:HEAD_DIM + 1, :]
        o_ref[0, :, h * HEAD_DIM:(h + 1) * HEAD_DIM] = out_t.T.astype(o_ref.dtype)


def _moba_attention(qt, k, vt, kmean, shift_rows, *, fixed_shift):
    batch, n_heads, n_blocks, _, _ = qt.shape
    seq = n_blocks * BLOCK
    k5 = k.reshape(batch, n_heads, n_blocks, BLOCK, HEAD_DIM)
    own_block = lambda b, nb: (b, 0, nb, 0, 0)
    stage_dtype = jnp.bfloat16 if fixed_shift else jnp.float32
    scratch = [
        pltpu.VMEM((n_heads, n_blocks, BLOCK, HEAD_DIM), jnp.bfloat16),
        pltpu.VMEM((n_heads, n_blocks, HEAD_DIM, BLOCK), jnp.bfloat16),
        pltpu.VMEM((n_heads, BLOCK, BLOCK), jnp.bfloat16),
        pltpu.VMEM((2, n_heads, BLOCK, BLOCK), stage_dtype),
        pltpu.VMEM((n_heads, HEAD_DIM + DENOM_ROWS, BLOCK), jnp.float32),
    ]
    if not fixed_shift:
        scratch += [
            pltpu.VMEM((2, n_heads, 1, BLOCK), jnp.float32),
            pltpu.VMEM((n_heads, 1, BLOCK), jnp.float32),
        ]
    return pl.pallas_call(
        functools.partial(_moba_kernel, n_blocks=n_blocks, fixed_shift=fixed_shift),
        out_shape=jax.ShapeDtypeStruct((batch, seq, D_MODEL), jnp.bfloat16),
        grid=(batch, n_blocks),
        in_specs=[
            pl.BlockSpec((1, n_heads, 1, HEAD_DIM, BLOCK), own_block),
            pl.BlockSpec((1, n_heads, 1, BLOCK, HEAD_DIM), own_block),
            pl.BlockSpec((1, n_heads, 1, HEAD_DIM, BLOCK), own_block),
            pl.BlockSpec((1, n_heads, n_blocks, HEAD_DIM), lambda b, nb: (b, 0, 0, 0)),
            _const_spec(shift_rows.shape),
        ],
        out_specs=pl.BlockSpec((1, BLOCK, D_MODEL), lambda b, nb: (b, nb, 0)),
        scratch_shapes=scratch,
        compiler_params=_params(2),
        name="moba_attn_fixed_shift" if fixed_shift else "moba_attn_online",
    )(qt, k5, vt, kmean, shift_rows)


def _attn_post_kernel(x_ref, o_ref, mod_ref, gain_ref, w_o_ref, w_gu_ref, w_down_ref, out_ref):
    mod = mod_ref[0, 0]
    y = jnp.dot(o_ref[0], w_o_ref[...], preferred_element_type=jnp.float32)
    x = x_ref[0] + mod[2:3] * y
    h = _modulated_norm(x, gain_ref[...], mod[3:4], mod[4:5]).astype(jnp.bfloat16)
    out_ref[0] = x + mod[5:6] * _swiglu(h, w_gu_ref, w_down_ref)


def _attn_post(x, o, mod, gain_ffn, w_o, w_gu, w_down, *, tm):
    batch, seq, _ = x.shape
    tile = lambda b, t: (b, t, 0)
    return pl.pallas_call(
        _attn_post_kernel,
        out_shape=jax.ShapeDtypeStruct(x.shape, x.dtype),
        grid=(batch, seq // tm),
        in_specs=[
            pl.BlockSpec((1, tm, D_MODEL), tile),
            pl.BlockSpec((1, tm, D_MODEL), tile),
            pl.BlockSpec((1, 1, 6, D_MODEL), lambda b, t: (0, b, 0, 0)),
            _const_spec((1, D_MODEL)),
            _const_spec(w_o.shape),
            _layer_spec(w_gu.shape, 0),
            _layer_spec(w_down.shape, 0),
        ],
        out_specs=pl.BlockSpec((1, tm, D_MODEL), tile),
        compiler_params=_params(2),
        name="attn_post",
    )(x, o, mod, gain_ffn, w_o, w_gu, w_down)


def _conv_layer_kernel(x_ref, mod_ref, gain_mix_ref, gain_ffn_ref, w_in_ref, conv_w_ref,
                       w_out_ref, w_gu_ref, w_down_ref, out_ref, u_ref, *, tm):
    mod = mod_ref[0, 0]
    h = _modulated_norm(x_ref[0], gain_mix_ref[...], mod[0:1], mod[1:2]).astype(jnp.bfloat16)

    @pl.when(pl.program_id(1) == 0)
    def _():
        u_ref[0:SUBLANES, :] = jnp.zeros((SUBLANES, D_MODEL), jnp.float32)

    @pl.when(pl.program_id(1) > 0)
    def _():
        u_ref[0:SUBLANES, :] = u_ref[tm:tm + SUBLANES, :]

    b_gate = jnp.dot(h, w_in_ref[:, :D_MODEL], preferred_element_type=jnp.float32)
    c_gate = jnp.dot(h, w_in_ref[:, D_MODEL:2 * D_MODEL], preferred_element_type=jnp.float32)
    u = c_gate * jnp.dot(h, w_in_ref[:, 2 * D_MODEL:], preferred_element_type=jnp.float32)
    u_ref[SUBLANES:, :] = u
    y = conv_w_ref[CONV_WIDTH - 1:CONV_WIDTH, :] * u
    for tap in range(CONV_WIDTH - 1):
        back = CONV_WIDTH - 1 - tap
        y = y + conv_w_ref[tap:tap + 1, :] * u_ref[SUBLANES - back:SUBLANES - back + tm, :]
    mix = jnp.dot((b_gate * y).astype(jnp.bfloat16), w_out_ref[...], preferred_element_type=jnp.float32)
    x = x_ref[0] + mod[2:3] * mix
    h = _modulated_norm(x, gain_ffn_ref[...], mod[3:4], mod[4:5]).astype(jnp.bfloat16)
    out_ref[0] = x + mod[5:6] * _swiglu(h, w_gu_ref, w_down_ref)


def _conv_layer(x, mod, gain_mix, gain_ffn, w_in, conv_w, w_out, w_gu, w_down, *, tm):
    batch, seq, _ = x.shape
    tile = lambda b, t: (b, t, 0)
    return pl.pallas_call(
        functools.partial(_conv_layer_kernel, tm=tm),
        out_shape=jax.ShapeDtypeStruct(x.shape, x.dtype),
        grid=(batch, seq // tm),
        in_specs=[
            pl.BlockSpec((1, tm, D_MODEL), tile),
            pl.BlockSpec((1, 1, 6, D_MODEL), lambda b, t: (1, b, 0, 0)),
            _const_spec((1, D_MODEL)),
            _const_spec((1, D_MODEL)),
            _const_spec(w_in.shape),
            _const_spec(conv_w.shape),
            _const_spec(w_out.shape),
            _layer_spec(w_gu.shape, 1),
            _layer_spec(w_down.shape, 1),
        ],
        out_specs=pl.BlockSpec((1, tm, D_MODEL), tile),
        scratch_shapes=[pltpu.VMEM((tm + SUBLANES, D_MODEL), jnp.float32)],
        compiler_params=_params(2),
        name="conv_layer",
    )(x, mod, gain_mix, gain_ffn, w_in, conv_w, w_out, w_gu, w_down)


def kernel(x, c, w_ada, b_ada, norm_mix, norm_ffn, w_qkv, w_o, q_gain, k_gain,
           w_in, conv_w, w_out, w_gate_up, w_down):
    batch, seq, _ = x.shape
    depth = w_ada.shape[0]
    assert depth == 2 and seq % BLOCK == 0
    bf16 = jnp.bfloat16
    mod = _adaln_mod(c, w_ada, b_ada).reshape(depth, batch, 6, D_MODEL)

    tm_pre, tm_ffn = 512, 512
    q, k, vt, kmean = _attn_pre(x, mod, norm_mix[0:1], w_qkv[0].astype(bf16), q_gain[0:1], k_gain[0:1],
                                tm=tm_pre)
    kmean = kmean.transpose(0, 2, 1, 3, 4).reshape(batch, N_HEADS, seq // BLOCK, HEAD_DIM)
    score_bound = HEAD_DIM * Q_SCALE * jnp.max(jnp.abs(q_gain[0])) * jnp.max(jnp.abs(k_gain[0]))
    shift_rows = jnp.zeros((SHIFT_ROWS, BLOCK), jnp.float32).at[0].set(-score_bound)
    o = lax.cond(
        score_bound <= MAX_FIXED_SHIFT,
        functools.partial(_moba_attention, fixed_shift=True),
        functools.partial(_moba_attention, fixed_shift=False),
        q, k, vt, kmean, shift_rows)
    w_gate_up, w_down = w_gate_up.astype(bf16), w_down.astype(bf16)
    x = _attn_post(x, o, mod, norm_ffn[0:1], w_o[0].astype(bf16), w_gate_up, w_down, tm=tm_ffn)
    x = _conv_layer(x, mod, norm_mix[1:2], norm_ffn[1:2], w_in[0].astype(bf16), conv_w[0],
                    w_out[0].astype(bf16), w_gate_up, w_down, tm=tm_ffn)
    return x
```

```python
import functools
import math

import jax
import jax.numpy as jnp
from jax import lax
from jax.experimental import pallas as pl
from jax.experimental.pallas import tpu as pltpu

D_MODEL = 1024
N_HEADS = 8
HEAD_DIM = D_MODEL // N_HEADS
BLOCK = 256
TOPK_BLOCKS = 3
CONV_WIDTH = 3
D_FF = int(math.ceil((8 * D_MODEL / 3) / 256) * 256)
EPS = 1e-6

MXU_COLS = 256
LANES = 128
FF_CHUNK = 2 * MXU_COLS
SUBLANES = 8
DENOM_ROWS = 16
SHIFT_ROWS = 16
LOOP_BLOCKS = 8
MAX_FIXED_SHIFT = 40.0
VMEM_LIMIT_BYTES = 56 * 1024 * 1024

MASKED = -1e30
Q_SCALE = HEAD_DIM ** -0.5 * math.log2(math.e)


def _const_spec(shape):
    zeros = (0,) * len(shape)
    return pl.BlockSpec(shape, lambda *_: zeros, pipeline_mode=pl.Buffered(1))


def _layer_spec(stacked_shape, layer):
    tail = (0,) * (len(stacked_shape) - 1)
    return pl.BlockSpec((1,) + tuple(stacked_shape[1:]), lambda *_: (layer,) + tail,
                        pipeline_mode=pl.Buffered(1))


def _params(n_grid_axes, flags=None):
    return pltpu.CompilerParams(
        dimension_semantics=("arbitrary",) * n_grid_axes,
        vmem_limit_bytes=VMEM_LIMIT_BYTES,
        flags=flags,
    )


def _modulated_norm(x, gain, shift, scale):
    y = x * lax.rsqrt(jnp.mean(x * x, axis=-1, keepdims=True) + EPS)
    return y * (gain * (1.0 + scale)) + shift


def _ff_chunks():
    chunks, start = [], 0
    while start < D_FF:
        size = min(FF_CHUNK, D_FF - start)
        chunks.append((start, size))
        start += size
    return chunks


def _swiglu(h_bf16, w_gu_ref, w_down_ref):
    acc = None
    for start, size in _ff_chunks():
        g = jnp.dot(h_bf16, w_gu_ref[0, :, start:start + size], preferred_element_type=jnp.float32)
        u = jnp.dot(h_bf16, w_gu_ref[0, :, D_FF + start:D_FF + start + size],
                    preferred_element_type=jnp.float32)
        act = (g * jax.nn.sigmoid(g) * u).astype(jnp.bfloat16)
        part = jnp.dot(act, w_down_ref[0, start:start + size, :], preferred_element_type=jnp.float32)
        acc = part if acc is None else acc + part
    return acc


def _adaln_kernel(c_ref, w_ref, b_ref, o_ref):
    c = c_ref[...]
    sc = (c * jax.nn.sigmoid(c)).astype(jnp.bfloat16)
    w = w_ref[0].astype(jnp.bfloat16)
    o_ref[0] = jnp.dot(sc, w, preferred_element_type=jnp.float32) + b_ref[0]


def _adaln_mod(c, w_ada, b_ada):
    depth, _, n_out = w_ada.shape
    batch = c.shape[0]
    n_chunk = n_out // 4
    return pl.pallas_call(
        _adaln_kernel,
        out_shape=jax.ShapeDtypeStruct((depth, batch, n_out), jnp.float32),
        grid=(depth, n_out // n_chunk),
        in_specs=[
            pl.BlockSpec((batch, D_MODEL), lambda i, j: (0, 0)),
            pl.BlockSpec((1, D_MODEL, n_chunk), lambda i, j: (i, 0, j)),
            pl.BlockSpec((1, 1, n_chunk), lambda i, j: (i, 0, j)),
        ],
        out_specs=pl.BlockSpec((1, batch, n_chunk), lambda i, j: (i, 0, j)),
        compiler_params=_params(2),
        name="adaln_mod",
    )(c, w_ada, b_ada.reshape(depth, 1, n_out))


def _attn_pre_kernel(x_ref, mod_ref, gain_ref, w_ref, qg_ref, kg_ref,
                     qt_ref, k_ref, vt_ref, kmean_ref, *, tm):
    mod = mod_ref[0, 0]
    h = _modulated_norm(x_ref[0], gain_ref[...], mod[0:1], mod[1:2]).astype(jnp.bfloat16)
    n_blk = tm // BLOCK
    heads_per_chunk = MXU_COLS // HEAD_DIM
    for chunk in range(3 * D_MODEL // MXU_COLS):
        r = jnp.dot(h, w_ref[:, chunk * MXU_COLS:(chunk + 1) * MXU_COLS],
                    preferred_element_type=jnp.float32)
        kind = (chunk * heads_per_chunk) // N_HEADS
        for half in range(heads_per_chunk):
            head = (chunk * heads_per_chunk) % N_HEADS + half
            t = r[:, half * HEAD_DIM:(half + 1) * HEAD_DIM]
            if kind == 2:
                for blk in range(n_blk):
                    vt_ref[0, head, blk] = t[blk * BLOCK:(blk + 1) * BLOCK].T.astype(jnp.bfloat16)
                continue
            tn = t * lax.rsqrt(jnp.mean(t * t, axis=-1, keepdims=True) + EPS)
            if kind == 0:
                qn = (tn * qg_ref[...]) * Q_SCALE
                for blk in range(n_blk):
                    qt_ref[0, head, blk] = qn[blk * BLOCK:(blk + 1) * BLOCK].T.astype(jnp.bfloat16)
            else:
                kn = tn * kg_ref[...]
                k_ref[0, head] = kn.astype(jnp.bfloat16)
                kmean_ref[0, 0, head] = jnp.mean(kn.reshape(n_blk, BLOCK, HEAD_DIM), axis=1)


def _attn_pre(x, mod, gain, w_qkv_bf16, q_gain, k_gain, *, tm):
    batch, seq, _ = x.shape
    n_tiles = seq // tm
    n_blk = tm // BLOCK
    t_shape = jax.ShapeDtypeStruct((batch, N_HEADS, seq // BLOCK, HEAD_DIM, BLOCK), jnp.bfloat16)
    t_spec = pl.BlockSpec((1, N_HEADS, n_blk, HEAD_DIM, BLOCK), lambda b, t: (b, 0, t, 0, 0))
    return pl.pallas_call(
        functools.partial(_attn_pre_kernel, tm=tm),
        out_shape=(
            t_shape,
            jax.ShapeDtypeStruct((batch, N_HEADS, seq, HEAD_DIM), jnp.bfloat16),
            t_shape,
            jax.ShapeDtypeStruct((batch, n_tiles, N_HEADS, n_blk, HEAD_DIM), jnp.float32),
        ),
        grid=(batch, n_tiles),
        in_specs=[
            pl.BlockSpec((1, tm, D_MODEL), lambda b, t: (b, t, 0)),
            pl.BlockSpec((1, 1, 6, D_MODEL), lambda b, t: (0, b, 0, 0)),
            _const_spec((1, D_MODEL)),
            _const_spec((D_MODEL, 3 * D_MODEL)),
            _const_spec((1, HEAD_DIM)),
            _const_spec((1, HEAD_DIM)),
        ],
        out_specs=(
            t_spec,
            pl.BlockSpec((1, N_HEADS, tm, HEAD_DIM), lambda b, t: (b, 0, t, 0)),
            t_spec,
            pl.BlockSpec((1, 1, N_HEADS, n_blk, HEAD_DIM), lambda b, t: (b, t, 0, 0, 0)),
        ),
        compiler_params=_params(2),
        name="attn_pre",
    )(x, mod, gain, w_qkv_bf16, q_gain, k_gain)


def _moba_kernel(qt_ref, k_blk_ref, vt_blk_ref, kmean_ref, shift_ref, o_ref,
                 k_ref, vt_ref, q2t_ref, stage_ref, acc_ref, *stat_refs, n_blocks, fixed_shift):
    nb = pl.program_id(1)
    for h in range(N_HEADS):
        k_ref[h, nb] = k_blk_ref[0, h, 0]
        vt_ref[h, nb] = vt_blk_ref[0, h, 0]
    blk_id = lax.broadcasted_iota(jnp.int32, (n_blocks, BLOCK), 0).astype(jnp.float32)
    key_pos = lax.broadcasted_iota(jnp.int32, (BLOCK, BLOCK), 0)
    qry_pos = lax.broadcasted_iota(jnp.int32, (BLOCK, BLOCK), 1)
    lane_id = lax.broadcasted_iota(jnp.int32, (BLOCK, LANES), 1)
    neg_inf = jnp.float32(-jnp.inf)
    zero_rows = jnp.zeros((BLOCK - HEAD_DIM - n_blocks - SHIFT_ROWS, BLOCK), jnp.bfloat16)
    ones_rows = jnp.ones((DENOM_ROWS, BLOCK), jnp.bfloat16)
    if not fixed_shift:
        mb_ref, m_ref = stat_refs

    def route_scores(h):
        kmean = kmean_ref[0, h]
        km_hi = kmean.astype(jnp.bfloat16)
        km_lo = (kmean - km_hi.astype(jnp.float32)).astype(jnp.bfloat16)
        r2 = jnp.dot(jnp.concatenate([km_hi, km_lo], axis=0), qt_ref[0, h, 0],
                     preferred_element_type=jnp.float32)
        return r2[:n_blocks] + r2[n_blocks:]

    def route_mask(score):
        score = jnp.where(blk_id < nb.astype(jnp.float32), score, neg_inf)
        mask = jnp.full((n_blocks, BLOCK), MASKED, jnp.float32)
        for _ in range(TOPK_BLOCKS):
            best = jnp.max(score, axis=0, keepdims=True)
            first = jnp.min(jnp.where(score == best, blk_id, float(n_blocks)), axis=0, keepdims=True)
            pick = blk_id == first
            mask = jnp.where(pick & (best > neg_inf), 0.0, mask)
            score = jnp.where(pick, neg_inf, score)
        return mask

    def key_extension(j):
        hit = None if j is None else lane_id == j
        if fixed_shift:
            hit = lane_id == n_blocks if hit is None else hit | (lane_id == n_blocks)
        if hit is None:
            return jnp.zeros((BLOCK, LANES), jnp.bfloat16)
        return jnp.where(hit, 1.0, 0.0).astype(jnp.bfloat16)

    def produce(h, j, extension, slot, own):
        keys = jnp.concatenate([k_ref[h, j], extension], axis=1)
        s = jnp.dot(keys, q2t_ref[h], preferred_element_type=jnp.float32)
        if own:
            s = jnp.where(key_pos <= qry_pos, s, MASKED)
        if fixed_shift:
            stage_ref[slot, h] = jnp.exp2(s).astype(jnp.bfloat16)
        else:
            stage_ref[slot, h] = s
            mb_ref[slot, h] = jnp.max(s, axis=0, keepdims=True)

    def consume(h, j, slot, first):
        values = jnp.concatenate([vt_ref[h, j], ones_rows], axis=0)
        if fixed_shift:
            pv = jnp.dot(values, stage_ref[slot, h], preferred_element_type=jnp.float32)
            acc_ref[h] = pv if first else acc_ref[h] + pv
            return
        m_blk = mb_ref[slot, h]
        if first:
            m_new = m_blk
        else:
            m = m_ref[h]
            m_new = jnp.maximum(m, m_blk)
            alpha = jnp.exp2(m - m_new)
        p = jnp.exp2(stage_ref[slot, h] - m_new).astype(jnp.bfloat16)
        pv = jnp.dot(values, p, preferred_element_type=jnp.float32)
        m_ref[h] = m_new
        acc_ref[h] = pv if first else alpha * acc_ref[h] + pv

    def run_blocks(first_block, count):
        for t in range(count):
            j = first_block + t
            j_next = jnp.minimum(j + 1, n_blocks - 1)
            extension = key_extension(j_next)
            for h in range(N_HEADS):
                produce(h, j_next, extension, (t + 1) % 2, False)
                consume(h, j, t % 2, False)

    unrouted = jnp.concatenate([jnp.zeros((n_blocks, BLOCK), jnp.float32), shift_ref[...]],
                               axis=0).astype(jnp.bfloat16)
    for h in range(N_HEADS):
        q2t_ref[h] = jnp.concatenate([qt_ref[0, h, 0], unrouted, zero_rows], axis=0)
    routing_scores = [route_scores(h) for h in range(N_HEADS)]
    extension = key_extension(None)
    for h in range(N_HEADS):
        produce(h, nb, extension, 1, True)
    for h in range(N_HEADS):
        q2t_ref[h, HEAD_DIM:HEAD_DIM + n_blocks, :] = route_mask(routing_scores[h]).astype(jnp.bfloat16)
    extension = key_extension(0)
    for h in range(N_HEADS):
        produce(h, 0, extension, 0, False)
        consume(h, nb, 1, True)

    def block_group(i, _):
        run_blocks(LOOP_BLOCKS * i, LOOP_BLOCKS)
        return 0

    n_groups = lax.shift_right_logical(nb, LOOP_BLOCKS.bit_length() - 1)
    lax.fori_loop(0, n_groups, block_group, 0)
    done = n_groups * LOOP_BLOCKS
    count = LOOP_BLOCKS // 2
    while count:
        @pl.when(jnp.bitwise_and(nb, count) != 0)
        def _(first_block=done, count=count):
            run_blocks(first_block, count)
        done = done + jnp.bitwise_and(nb, count)
        count //= 2
    for h in range(N_HEADS):
        out_t = acc_ref[h, 0:HEAD_DIM, :] / acc_ref[h, HEAD_DIM:HEAD_DIM + 1, :]
        o_ref[0, :, h * HEAD_DIM:(h + 1) * HEAD_DIM] = out_t.T.astype(o_ref.dtype)


def _moba_attention(qt, k, vt, kmean, shift_rows, *, fixed_shift):
    batch, n_heads, n_blocks, _, _ = qt.shape
    seq = n_blocks * BLOCK
    k5 = k.reshape(batch, n_heads, n_blocks, BLOCK, HEAD_DIM)
    own_block = lambda b, nb: (b, 0, nb, 0, 0)
    stage_dtype = jnp.bfloat16 if fixed_shift else jnp.float32
    scratch = [
        pltpu.VMEM((n_heads, n_blocks, BLOCK, HEAD_DIM), jnp.bfloat16),
        pltpu.VMEM((n_heads, n_blocks, HEAD_DIM, BLOCK), jnp.bfloat16),
        pltpu.VMEM((n_heads, BLOCK, BLOCK), jnp.bfloat16),
        pltpu.VMEM((2, n_heads, BLOCK, BLOCK), stage_dtype),
        pltpu.VMEM((n_heads, HEAD_DIM + DENOM_ROWS, BLOCK), jnp.float32),
    ]
    if not fixed_shift:
        scratch += [
            pltpu.VMEM((2, n_heads, 1, BLOCK), jnp.float32),
            pltpu.VMEM((n_heads, 1, BLOCK), jnp.float32),
        ]
    return pl.pallas_call(
        functools.partial(_moba_kernel, n_blocks=n_blocks, fixed_shift=fixed_shift),
        out_shape=jax.ShapeDtypeStruct((batch, seq, D_MODEL), jnp.bfloat16),
        grid=(batch, n_blocks),
        in_specs=[
            pl.BlockSpec((1, n_heads, 1, HEAD_DIM, BLOCK), own_block),
            pl.BlockSpec((1, n_heads, 1, BLOCK, HEAD_DIM), own_block),
            pl.BlockSpec((1, n_heads, 1, HEAD_DIM, BLOCK), own_block),
            pl.BlockSpec((1, n_heads, n_blocks, HEAD_DIM), lambda b, nb: (b, 0, 0, 0)),
            _const_spec(shift_rows.shape),
        ],
        out_specs=pl.BlockSpec((1, BLOCK, D_MODEL), lambda b, nb: (b, nb, 0)),
        scratch_shapes=scratch,
        compiler_params=_params(2),
        name="moba_attn_fixed_shift" if fixed_shift else "moba_attn_online",
    )(qt, k5, vt, kmean, shift_rows)


def _attn_post_kernel(x_ref, o_ref, mod_ref, gain_ref, w_o_ref, w_gu_ref, w_down_ref, out_ref):
    mod = mod_ref[0, 0]
    y = jnp.dot(o_ref[0], w_o_ref[...], preferred_element_type=jnp.float32)
    x = x_ref[0] + mod[2:3] * y
    h = _modulated_norm(x, gain_ref[...], mod[3:4], mod[4:5]).astype(jnp.bfloat16)
    out_ref[0] = x + mod[5:6] * _swiglu(h, w_gu_ref, w_down_ref)


def _attn_post(x, o, mod, gain_ffn, w_o, w_gu, w_down, *, tm):
    batch, seq, _ = x.shape
    tile = lambda b, t: (b, t, 0)
    return pl.pallas_call(
        _attn_post_kernel,
        out_shape=jax.ShapeDtypeStruct(x.shape, x.dtype),
        grid=(batch, seq // tm),
        in_specs=[
            pl.BlockSpec((1, tm, D_MODEL), tile),
            pl.BlockSpec((1, tm, D_MODEL), tile),
            pl.BlockSpec((1, 1, 6, D_MODEL), lambda b, t: (0, b, 0, 0)),
            _const_spec((1, D_MODEL)),
            _const_spec(w_o.shape),
            _layer_spec(w_gu.shape, 0),
            _layer_spec(w_down.shape, 0),
        ],
        out_specs=pl.BlockSpec((1, tm, D_MODEL), tile),
        compiler_params=_params(2),
        name="attn_post",
    )(x, o, mod, gain_ffn, w_o, w_gu, w_down)


def _conv_layer_kernel(x_ref, mod_ref, gain_mix_ref, gain_ffn_ref, w_in_ref, conv_w_ref,
                       w_out_ref, w_gu_ref, w_down_ref, out_ref, u_ref, *, tm):
    mod = mod_ref[0, 0]
    h = _modulated_norm(x_ref[0], gain_mix_ref[...], mod[0:1], mod[1:2]).astype(jnp.bfloat16)

    @pl.when(pl.program_id(1) == 0)
    def _():
        u_ref[0:SUBLANES, :] = jnp.zeros((SUBLANES, D_MODEL), jnp.float32)

    @pl.when(pl.program_id(1) > 0)
    def _():
        u_ref[0:SUBLANES, :] = u_ref[tm:tm + SUBLANES, :]

    b_gate = jnp.dot(h, w_in_ref[:, :D_MODEL], preferred_element_type=jnp.float32)
    c_gate = jnp.dot(h, w_in_ref[:, D_MODEL:2 * D_MODEL], preferred_element_type=jnp.float32)
    u = c_gate * jnp.dot(h, w_in_ref[:, 2 * D_MODEL:], preferred_element_type=jnp.float32)
    u_ref[SUBLANES:, :] = u
    y = conv_w_ref[CONV_WIDTH - 1:CONV_WIDTH, :] * u
    for tap in range(CONV_WIDTH - 1):
        back = CONV_WIDTH - 1 - tap
        y = y + conv_w_ref[tap:tap + 1, :] * u_ref[SUBLANES - back:SUBLANES - back + tm, :]
    mix = jnp.dot((b_gate * y).astype(jnp.bfloat16), w_out_ref[...], preferred_element_type=jnp.float32)
    x = x_ref[0] + mod[2:3] * mix
    h = _modulated_norm(x, gain_ffn_ref[...], mod[3:4], mod[4:5]).astype(jnp.bfloat16)
    out_ref[0] = x + mod[5:6] * _swiglu(h, w_gu_ref, w_down_ref)


def _conv_layer(x, mod, gain_mix, gain_ffn, w_in, conv_w, w_out, w_gu, w_down, *, tm):
    batch, seq, _ = x.shape
    tile = lambda b, t: (b, t, 0)
    return pl.pallas_call(
        functools.partial(_conv_layer_kernel, tm=tm),
        out_shape=jax.ShapeDtypeStruct(x.shape, x.dtype),
        grid=(batch, seq // tm),
        in_specs=[
            pl.BlockSpec((1, tm, D_MODEL), tile),
            pl.BlockSpec((1, 1, 6, D_MODEL), lambda b, t: (1, b, 0, 0)),
            _const_spec((1, D_MODEL)),
            _const_spec((1, D_MODEL)),
            _const_spec(w_in.shape),
            _const_spec(conv_w.shape),
            _const_spec(w_out.shape),
            _layer_spec(w_gu.shape, 1),
            _layer_spec(w_down.shape, 1),
        ],
        out_specs=pl.BlockSpec((1, tm, D_MODEL), tile),
        scratch_shapes=[pltpu.VMEM((tm + SUBLANES, D_MODEL), jnp.float32)],
        compiler_params=_params(2),
        name="conv_layer",
    )(x, mod, gain_mix, gain_ffn, w_in, conv_w, w_out, w_gu, w_down)


def kernel(x, c, w_ada, b_ada, norm_mix, norm_ffn, w_qkv, w_o, q_gain, k_gain,
           w_in, conv_w, w_out, w_gate_up, w_down):
    batch, seq, _ = x.shape
    depth = w_ada.shape[0]
    assert depth == 2 and seq % BLOCK == 0
    bf16 = jnp.bfloat16
    mod = _adaln_mod(c, w_ada, b_ada).reshape(depth, batch, 6, D_MODEL)

    tm_pre, tm_post, tm_conv = 1024, 1024, 512
    q, k, vt, kmean = _attn_pre(x, mod, norm_mix[0:1], w_qkv[0].astype(bf16), q_gain[0:1], k_gain[0:1],
                                tm=tm_pre)
    kmean = kmean.transpose(0, 2, 1, 3, 4).reshape(batch, N_HEADS, seq // BLOCK, HEAD_DIM)
    score_bound = HEAD_DIM * Q_SCALE * jnp.max(jnp.abs(q_gain[0])) * jnp.max(jnp.abs(k_gain[0]))
    shift_rows = jnp.zeros((SHIFT_ROWS, BLOCK), jnp.float32).at[0].set(-score_bound)
    o = lax.cond(
        score_bound <= MAX_FIXED_SHIFT,
        functools.partial(_moba_attention, fixed_shift=True),
        functools.partial(_moba_attention, fixed_shift=False),
        q, k, vt, kmean, shift_rows)
    w_gate_up, w_down = w_gate_up.astype(bf16), w_down.astype(bf16)
    x = _attn_post(x, o, mod, norm_ffn[0:1], w_o[0].astype(bf16), w_gate_up, w_down, tm=tm_post)
    x = _conv_layer(x, mod, norm_mix[1:2], norm_ffn[1:2], w_in[0].astype(bf16), conv_w[0],
                    w_out[0].astype(bf16), w_gate_up, w_down, tm=tm_conv)
    return x
```

```python
import functools
import math

import jax
import jax.numpy as jnp
from jax import lax
from jax.experimental import pallas as pl
from jax.experimental.pallas import tpu as pltpu

D_MODEL = 1024
N_HEADS = 8
HEAD_DIM = D_MODEL // N_HEADS
BLOCK = 256
TOPK_BLOCKS = 3
CONV_WIDTH = 3
D_FF = int(math.ceil((8 * D_MODEL / 3) / 256) * 256)
EPS = 1e-6

MXU_COLS = 256
LANES = 128
FF_CHUNK = 2 * MXU_COLS
SUBLANES = 8
DENOM_ROWS = 16
SHIFT_ROWS = 16
LOOP_BLOCKS = 8
QUERY_BLOCKS_PER_STEP = 2
ADALN_COL_CHUNKS = 4
MAX_FIXED_SHIFT = 40.0
VMEM_LIMIT_BYTES = 56 * 1024 * 1024

MASKED = -1e30
Q_SCALE = HEAD_DIM ** -0.5 * math.log2(math.e)


def _const_spec(shape):
    zeros = (0,) * len(shape)
    return pl.BlockSpec(shape, lambda *_: zeros, pipeline_mode=pl.Buffered(1))


def _layer_spec(stacked_shape, layer):
    tail = (0,) * (len(stacked_shape) - 1)
    return pl.BlockSpec((1,) + tuple(stacked_shape[1:]), lambda *_: (layer,) + tail,
                        pipeline_mode=pl.Buffered(1))


def _params(n_grid_axes):
    return pltpu.CompilerParams(
        dimension_semantics=("arbitrary",) * n_grid_axes,
        vmem_limit_bytes=VMEM_LIMIT_BYTES,
    )


def _modulated_norm(x, gain, shift, scale):
    y = x * lax.rsqrt(jnp.mean(x * x, axis=-1, keepdims=True) + EPS)
    return y * (gain * (1.0 + scale)) + shift


def _ff_chunks():
    chunks, start = [], 0
    while start < D_FF:
        size = min(FF_CHUNK, D_FF - start)
        chunks.append((start, size))
        start += size
    return chunks


def _swiglu(h_bf16, w_gu_ref, w_down_ref):
    acc = None
    for start, size in _ff_chunks():
        g = jnp.dot(h_bf16, w_gu_ref[0, :, start:start + size], preferred_element_type=jnp.float32)
        u = jnp.dot(h_bf16, w_gu_ref[0, :, D_FF + start:D_FF + start + size],
                    preferred_element_type=jnp.float32)
        act = (g * jax.nn.sigmoid(g) * u).astype(jnp.bfloat16)
        part = jnp.dot(act, w_down_ref[0, start:start + size, :], preferred_element_type=jnp.float32)
        acc = part if acc is None else acc + part
    return acc


def _adaln_kernel(c_ref, w_ref, b_ref, o_ref):
    c = c_ref[...]
    sc = (c * jax.nn.sigmoid(c)).astype(jnp.bfloat16)
    w = w_ref[0].astype(jnp.bfloat16)
    o_ref[0] = jnp.dot(sc, w, preferred_element_type=jnp.float32) + b_ref[0]


def _adaln_mod(c, w_ada, b_ada):
    depth, _, n_out = w_ada.shape
    batch = c.shape[0]
    n_chunk = n_out // ADALN_COL_CHUNKS
    return pl.pallas_call(
        _adaln_kernel,
        out_shape=jax.ShapeDtypeStruct((depth, batch, n_out), jnp.float32),
        grid=(depth, n_out // n_chunk),
        in_specs=[
            pl.BlockSpec((batch, D_MODEL), lambda i, j: (0, 0)),
            pl.BlockSpec((1, D_MODEL, n_chunk), lambda i, j: (i, 0, j)),
            pl.BlockSpec((1, 1, n_chunk), lambda i, j: (i, 0, j)),
        ],
        out_specs=pl.BlockSpec((1, batch, n_chunk), lambda i, j: (i, 0, j)),
        compiler_params=_params(2),
        name="adaln_mod",
    )(c, w_ada, b_ada.reshape(depth, 1, n_out))


def _attn_pre_kernel(x_ref, mod_ref, gain_ref, w_ref, qg_ref, kg_ref,
                     qt_ref, k_ref, vt_ref, kmean_ref, *, tm):
    mod = mod_ref[0, 0]
    h = _modulated_norm(x_ref[0], gain_ref[...], mod[0:1], mod[1:2]).astype(jnp.bfloat16)
    n_blk = tm // BLOCK
    heads_per_chunk = MXU_COLS // HEAD_DIM
    for chunk in range(3 * D_MODEL // MXU_COLS):
        r = jnp.dot(h, w_ref[:, chunk * MXU_COLS:(chunk + 1) * MXU_COLS],
                    preferred_element_type=jnp.float32)
        kind = (chunk * heads_per_chunk) // N_HEADS
        for half in range(heads_per_chunk):
            head = (chunk * heads_per_chunk) % N_HEADS + half
            t = r[:, half * HEAD_DIM:(half + 1) * HEAD_DIM]
            if kind == 2:
                for blk in range(n_blk):
                    vt_ref[0, head, blk] = t[blk * BLOCK:(blk + 1) * BLOCK].T.astype(jnp.bfloat16)
                continue
            tn = t * lax.rsqrt(jnp.mean(t * t, axis=-1, keepdims=True) + EPS)
            if kind == 0:
                qn = (tn * qg_ref[...]) * Q_SCALE
                for blk in range(n_blk):
                    qt_ref[0, head, blk] = qn[blk * BLOCK:(blk + 1) * BLOCK].T.astype(jnp.bfloat16)
            else:
                kn = tn * kg_ref[...]
                k_ref[0, head] = kn.astype(jnp.bfloat16)
                kmean_ref[0, 0, head] = jnp.mean(kn.reshape(n_blk, BLOCK, HEAD_DIM), axis=1)


def _attn_pre(x, mod, gain, w_qkv_bf16, q_gain, k_gain, *, tm):
    batch, seq, _ = x.shape
    n_tiles = seq // tm
    n_blk = tm // BLOCK
    t_shape = jax.ShapeDtypeStruct((batch, N_HEADS, seq // BLOCK, HEAD_DIM, BLOCK), jnp.bfloat16)
    t_spec = pl.BlockSpec((1, N_HEADS, n_blk, HEAD_DIM, BLOCK), lambda b, t: (b, 0, t, 0, 0))
    return pl.pallas_call(
        functools.partial(_attn_pre_kernel, tm=tm),
        out_shape=(
            t_shape,
            jax.ShapeDtypeStruct((batch, N_HEADS, seq, HEAD_DIM), jnp.bfloat16),
            t_shape,
            jax.ShapeDtypeStruct((batch, n_tiles, N_HEADS, n_blk, HEAD_DIM), jnp.float32),
        ),
        grid=(batch, n_tiles),
        in_specs=[
            pl.BlockSpec((1, tm, D_MODEL), lambda b, t: (b, t, 0)),
            pl.BlockSpec((1, 1, 6, D_MODEL), lambda b, t: (0, b, 0, 0)),
            _const_spec((1, D_MODEL)),
            _const_spec((D_MODEL, 3 * D_MODEL)),
            _const_spec((1, HEAD_DIM)),
            _const_spec((1, HEAD_DIM)),
        ],
        out_specs=(
            t_spec,
            pl.BlockSpec((1, N_HEADS, tm, HEAD_DIM), lambda b, t: (b, 0, t, 0)),
            t_spec,
            pl.BlockSpec((1, 1, N_HEADS, n_blk, HEAD_DIM), lambda b, t: (b, t, 0, 0, 0)),
        ),
        compiler_params=_params(2),
        name="attn_pre",
    )(x, mod, gain, w_qkv_bf16, q_gain, k_gain)


def _moba_kernel(*refs, n_blocks, fixed_shift):
    for half in range(QUERY_BLOCKS_PER_STEP):
        _attend_query_block(pl.program_id(1) * QUERY_BLOCKS_PER_STEP + half, half, *refs,
                            n_blocks=n_blocks, fixed_shift=fixed_shift)


def _attend_query_block(nb, half, qt_ref, k_blk_ref, vt_blk_ref, kmean_ref, shift_ref, o_ref,
                        k_ref, vt_ref, q2t_ref, stage_ref, acc_ref, *stat_refs, n_blocks, fixed_shift):
    for h in range(N_HEADS):
        k_ref[h, nb] = k_blk_ref[0, h, half]
        vt_ref[h, nb] = vt_blk_ref[0, h, half]
    blk_id = lax.broadcasted_iota(jnp.int32, (n_blocks, BLOCK), 0).astype(jnp.float32)
    key_pos = lax.broadcasted_iota(jnp.int32, (BLOCK, BLOCK), 0)
    qry_pos = lax.broadcasted_iota(jnp.int32, (BLOCK, BLOCK), 1)
    lane_id = lax.broadcasted_iota(jnp.int32, (BLOCK, LANES), 1)
    neg_inf = jnp.float32(-jnp.inf)
    zero_rows = jnp.zeros((BLOCK - HEAD_DIM - n_blocks - SHIFT_ROWS, BLOCK), jnp.bfloat16)
    ones_rows = jnp.ones((DENOM_ROWS, BLOCK), jnp.bfloat16)
    if not fixed_shift:
        mb_ref, m_ref = stat_refs

    def route_scores(h):
        kmean = kmean_ref[0, h]
        km_hi = kmean.astype(jnp.bfloat16)
        km_lo = (kmean - km_hi.astype(jnp.float32)).astype(jnp.bfloat16)
        r2 = jnp.dot(jnp.concatenate([km_hi, km_lo], axis=0), qt_ref[0, h, half],
                     preferred_element_type=jnp.float32)
        return r2[:n_blocks] + r2[n_blocks:]

    def route_mask(score):
        score = jnp.where(blk_id < nb.astype(jnp.float32), score, neg_inf)
        mask = jnp.full((n_blocks, BLOCK), MASKED, jnp.float32)
        for _ in range(TOPK_BLOCKS):
            best = jnp.max(score, axis=0, keepdims=True)
            first = jnp.min(jnp.where(score == best, blk_id, float(n_blocks)), axis=0, keepdims=True)
            pick = blk_id == first
            mask = jnp.where(pick & (best > neg_inf), 0.0, mask)
            score = jnp.where(pick, neg_inf, score)
        return mask

    def key_extension(j):
        hit = None if j is None else lane_id == j
        if fixed_shift:
            hit = lane_id == n_blocks if hit is None else hit | (lane_id == n_blocks)
        if hit is None:
            return jnp.zeros((BLOCK, LANES), jnp.bfloat16)
        return jnp.where(hit, 1.0, 0.0).astype(jnp.bfloat16)

    def produce(h, j, extension, slot, own):
        keys = jnp.concatenate([k_ref[h, j], extension], axis=1)
        s = jnp.dot(keys, q2t_ref[h], preferred_element_type=jnp.float32)
        if own:
            s = jnp.where(key_pos <= qry_pos, s, MASKED)
        if fixed_shift:
            stage_ref[slot, h] = jnp.exp2(s).astype(jnp.bfloat16)
        else:
            stage_ref[slot, h] = s
            mb_ref[slot, h] = jnp.max(s, axis=0, keepdims=True)

    def consume(h, j, slot, first):
        values = jnp.concatenate([vt_ref[h, j], ones_rows], axis=0)
        if fixed_shift:
            pv = jnp.dot(values, stage_ref[slot, h], preferred_element_type=jnp.float32)
            acc_ref[h] = pv if first else acc_ref[h] + pv
            return
        m_blk = mb_ref[slot, h]
        if first:
            m_new = m_blk
        else:
            m = m_ref[h]
            m_new = jnp.maximum(m, m_blk)
            alpha = jnp.exp2(m - m_new)
        p = jnp.exp2(stage_ref[slot, h] - m_new).astype(jnp.bfloat16)
        pv = jnp.dot(values, p, preferred_element_type=jnp.float32)
        m_ref[h] = m_new
        acc_ref[h] = pv if first else alpha * acc_ref[h] + pv

    def run_blocks(first_block, count):
        for t in range(count):
            j = first_block + t
            j_next = jnp.minimum(j + 1, n_blocks - 1)
            extension = key_extension(j_next)
            for h in range(N_HEADS):
                produce(h, j_next, extension, (t + 1) % 2, False)
                consume(h, j, t % 2, False)

    unrouted = jnp.concatenate([jnp.zeros((n_blocks, BLOCK), jnp.float32), shift_ref[...]],
                               axis=0).astype(jnp.bfloat16)
    for h in range(N_HEADS):
        q2t_ref[h] = jnp.concatenate([qt_ref[0, h, half], unrouted, zero_rows], axis=0)
    routing_scores = [route_scores(h) for h in range(N_HEADS)]
    extension = key_extension(None)
    for h in range(N_HEADS):
        produce(h, nb, extension, 1, True)
    for h in range(N_HEADS):
        q2t_ref[h, HEAD_DIM:HEAD_DIM + n_blocks, :] = route_mask(routing_scores[h]).astype(jnp.bfloat16)
    extension = key_extension(0)
    for h in range(N_HEADS):
        produce(h, 0, extension, 0, False)
        consume(h, nb, 1, True)

    def block_group(i, _):
        run_blocks(LOOP_BLOCKS * i, LOOP_BLOCKS)
        return 0

    n_groups = lax.shift_right_logical(nb, LOOP_BLOCKS.bit_length() - 1)
    lax.fori_loop(0, n_groups, block_group, 0)
    done = n_groups * LOOP_BLOCKS
    count = LOOP_BLOCKS // 2
    while count:
        @pl.when(jnp.bitwise_and(nb, count) != 0)
        def _(first_block=done, count=count):
            run_blocks(first_block, count)
        done = done + jnp.bitwise_and(nb, count)
        count //= 2
    for h in range(N_HEADS):
        out_t = acc_ref[h, 0:HEAD_DIM, :] / acc_ref[h, HEAD_DIM:HEAD_DIM + 1, :]
        o_ref[0, half * BLOCK:(half + 1) * BLOCK, h * HEAD_DIM:(h + 1) * HEAD_DIM] = (
            out_t.T.astype(o_ref.dtype))


def _moba_attention(qt, k, vt, kmean, shift_rows, *, fixed_shift):
    batch, n_heads, n_blocks, _, _ = qt.shape
    seq = n_blocks * BLOCK
    k5 = k.reshape(batch, n_heads, n_blocks, BLOCK, HEAD_DIM)
    per_step = QUERY_BLOCKS_PER_STEP
    own_blocks = lambda b, g: (b, 0, g, 0, 0)
    stage_dtype = jnp.bfloat16 if fixed_shift else jnp.float32
    scratch = [
        pltpu.VMEM((n_heads, n_blocks, BLOCK, HEAD_DIM), jnp.bfloat16),
        pltpu.VMEM((n_heads, n_blocks, HEAD_DIM, BLOCK), jnp.bfloat16),
        pltpu.VMEM((n_heads, BLOCK, BLOCK), jnp.bfloat16),
        pltpu.VMEM((2, n_heads, BLOCK, BLOCK), stage_dtype),
        pltpu.VMEM((n_heads, HEAD_DIM + DENOM_ROWS, BLOCK), jnp.float32),
    ]
    if not fixed_shift:
        scratch += [
            pltpu.VMEM((2, n_heads, 1, BLOCK), jnp.float32),
            pltpu.VMEM((n_heads, 1, BLOCK), jnp.float32),
        ]
    return pl.pallas_call(
        functools.partial(_moba_kernel, n_blocks=n_blocks, fixed_shift=fixed_shift),
        out_shape=jax.ShapeDtypeStruct((batch, seq, D_MODEL), jnp.bfloat16),
        grid=(batch, n_blocks // per_step),
        in_specs=[
            pl.BlockSpec((1, n_heads, per_step, HEAD_DIM, BLOCK), own_blocks),
            pl.BlockSpec((1, n_heads, per_step, BLOCK, HEAD_DIM), own_blocks),
            pl.BlockSpec((1, n_heads, per_step, HEAD_DIM, BLOCK), own_blocks),
            pl.BlockSpec((1, n_heads, n_blocks, HEAD_DIM), lambda b, g: (b, 0, 0, 0)),
            _const_spec(shift_rows.shape),
        ],
        out_specs=pl.BlockSpec((1, per_step * BLOCK, D_MODEL), lambda b, g: (b, g, 0)),
        scratch_shapes=scratch,
        compiler_params=_params(2),
        name="moba_attn_fixed_shift" if fixed_shift else "moba_attn_online",
    )(qt, k5, vt, kmean, shift_rows)


def _attn_post_kernel(x_ref, o_ref, mod_ref, gain_ref, w_o_ref, w_gu_ref, w_down_ref, out_ref):
    mod = mod_ref[0, 0]
    y = jnp.dot(o_ref[0], w_o_ref[...], preferred_element_type=jnp.float32)
    x = x_ref[0] + mod[2:3] * y
    h = _modulated_norm(x, gain_ref[...], mod[3:4], mod[4:5]).astype(jnp.bfloat16)
    out_ref[0] = x + mod[5:6] * _swiglu(h, w_gu_ref, w_down_ref)


def _attn_post(x, o, mod, gain_ffn, w_o, w_gu, w_down, *, tm):
    batch, seq, _ = x.shape
    tile = lambda b, t: (b, t, 0)
    return pl.pallas_call(
        _attn_post_kernel,
        out_shape=jax.ShapeDtypeStruct(x.shape, x.dtype),
        grid=(batch, seq // tm),
        in_specs=[
            pl.BlockSpec((1, tm, D_MODEL), tile),
            pl.BlockSpec((1, tm, D_MODEL), tile),
            pl.BlockSpec((1, 1, 6, D_MODEL), lambda b, t: (0, b, 0, 0)),
            _const_spec((1, D_MODEL)),
            _const_spec(w_o.shape),
            _layer_spec(w_gu.shape, 0),
            _layer_spec(w_down.shape, 0),
        ],
        out_specs=pl.BlockSpec((1, tm, D_MODEL), tile),
        compiler_params=_params(2),
        name="attn_post",
    )(x, o, mod, gain_ffn, w_o, w_gu, w_down)


def _conv_layer_kernel(x_ref, mod_ref, gain_mix_ref, gain_ffn_ref, w_in_ref, conv_w_ref,
                       w_out_ref, w_gu_ref, w_down_ref, out_ref, u_ref, *, tm):
    mod = mod_ref[0, 0]
    h = _modulated_norm(x_ref[0], gain_mix_ref[...], mod[0:1], mod[1:2]).astype(jnp.bfloat16)

    @pl.when(pl.program_id(1) == 0)
    def _():
        u_ref[0:SUBLANES, :] = jnp.zeros((SUBLANES, D_MODEL), jnp.float32)

    @pl.when(pl.program_id(1) > 0)
    def _():
        u_ref[0:SUBLANES, :] = u_ref[tm:tm + SUBLANES, :]

    b_gate = jnp.dot(h, w_in_ref[:, :D_MODEL], preferred_element_type=jnp.float32)
    c_gate = jnp.dot(h, w_in_ref[:, D_MODEL:2 * D_MODEL], preferred_element_type=jnp.float32)
    u = c_gate * jnp.dot(h, w_in_ref[:, 2 * D_MODEL:], preferred_element_type=jnp.float32)
    u_ref[SUBLANES:, :] = u
    y = conv_w_ref[CONV_WIDTH - 1:CONV_WIDTH, :] * u
    for tap in range(CONV_WIDTH - 1):
        back = CONV_WIDTH - 1 - tap
        y = y + conv_w_ref[tap:tap + 1, :] * u_ref[SUBLANES - back:SUBLANES - back + tm, :]
    mix = jnp.dot((b_gate * y).astype(jnp.bfloat16), w_out_ref[...], preferred_element_type=jnp.float32)
    x = x_ref[0] + mod[2:3] * mix
    h = _modulated_norm(x, gain_ffn_ref[...], mod[3:4], mod[4:5]).astype(jnp.bfloat16)
    out_ref[0] = x + mod[5:6] * _swiglu(h, w_gu_ref, w_down_ref)


def _conv_layer(x, mod, gain_mix, gain_ffn, w_in, conv_w, w_out, w_gu, w_down, *, tm):
    batch, seq, _ = x.shape
    tile = lambda b, t: (b, t, 0)
    return pl.pallas_call(
        functools.partial(_conv_layer_kernel, tm=tm),
        out_shape=jax.ShapeDtypeStruct(x.shape, x.dtype),
        grid=(batch, seq // tm),
        in_specs=[
            pl.BlockSpec((1, tm, D_MODEL), tile),
            pl.BlockSpec((1, 1, 6, D_MODEL), lambda b, t: (1, b, 0, 0)),
            _const_spec((1, D_MODEL)),
            _const_spec((1, D_MODEL)),
            _const_spec(w_in.shape),
            _const_spec(conv_w.shape),
            _const_spec(w_out.shape),
            _layer_spec(w_gu.shape, 1),
            _layer_spec(w_down.shape, 1),
        ],
        out_specs=pl.BlockSpec((1, tm, D_MODEL), tile),
        scratch_shapes=[pltpu.VMEM((tm + SUBLANES, D_MODEL), jnp.float32)],
        compiler_params=_params(2),
        name="conv_layer",
    )(x, mod, gain_mix, gain_ffn, w_in, conv_w, w_out, w_gu, w_down)


def kernel(x, c, w_ada, b_ada, norm_mix, norm_ffn, w_qkv, w_o, q_gain, k_gain,
           w_in, conv_w, w_out, w_gate_up, w_down):
    batch, seq, _ = x.shape
    depth = w_ada.shape[0]
    assert depth == 2 and seq % BLOCK == 0
    bf16 = jnp.bfloat16
    mod = _adaln_mod(c, w_ada, b_ada).reshape(depth, batch, 6, D_MODEL)

    tm_pre, tm_post, tm_conv = 1024, 1024, 512
    q, k, vt, kmean = _attn_pre(x, mod, norm_mix[0:1], w_qkv[0].astype(bf16), q_gain[0:1], k_gain[0:1],
                                tm=tm_pre)
    kmean = kmean.transpose(0, 2, 1, 3, 4).reshape(batch, N_HEADS, seq // BLOCK, HEAD_DIM)
    score_bound = HEAD_DIM * Q_SCALE * jnp.max(jnp.abs(q_gain[0])) * jnp.max(jnp.abs(k_gain[0]))
    shift_rows = jnp.zeros((SHIFT_ROWS, BLOCK), jnp.float32).at[0].set(-score_bound)
    o = lax.cond(
        score_bound <= MAX_FIXED_SHIFT,
        functools.partial(_moba_attention, fixed_shift=True),
        functools.partial(_moba_attention, fixed_shift=False),
        q, k, vt, kmean, shift_rows)
    w_gate_up, w_down = w_gate_up.astype(bf16), w_down.astype(bf16)
    x = _attn_post(x, o, mod, norm_ffn[0:1], w_o[0].astype(bf16), w_gate_up, w_down, tm=tm_post)
    x = _conv_layer(x, mod, norm_mix[1:2], norm_ffn[1:2], w_in[0].astype(bf16), conv_w[0],
                    w_out[0].astype(bf16), w_gate_up, w_down, tm=tm_conv)
    return x
```

```python
import functools
import math

import jax
import jax.numpy as jnp
from jax import lax
from jax.experimental import pallas as pl
from jax.experimental.pallas import tpu as pltpu

D_MODEL = 1024
N_HEADS = 8
HEAD_DIM = D_MODEL // N_HEADS
BLOCK = 256
TOPK_BLOCKS = 3
CONV_WIDTH = 3
D_FF = int(math.ceil((8 * D_MODEL / 3) / 256) * 256)
EPS = 1e-6

MXU_COLS = 256
LANES = 128
FF_CHUNK = 2 * MXU_COLS
SUBLANES = 8
DENOM_ROWS = 16
SHIFT_ROWS = 16
LOOP_BLOCKS = 8
QUERY_BLOCKS_PER_STEP = 2
ADALN_COL_CHUNKS = 4
MAX_FIXED_SHIFT = 40.0
VMEM_LIMIT_BYTES = 60 * 1024 * 1024

MASKED = -1e30
Q_SCALE = HEAD_DIM ** -0.5 * math.log2(math.e)


def _const_spec(shape):
    zeros = (0,) * len(shape)
    return pl.BlockSpec(shape, lambda *_: zeros, pipeline_mode=pl.Buffered(1))


def _layer_spec(stacked_shape, layer):
    tail = (0,) * (len(stacked_shape) - 1)
    return pl.BlockSpec((1,) + tuple(stacked_shape[1:]), lambda *_: (layer,) + tail,
                        pipeline_mode=pl.Buffered(1))


def _params(n_grid_axes):
    return pltpu.CompilerParams(
        dimension_semantics=("arbitrary",) * n_grid_axes,
        vmem_limit_bytes=VMEM_LIMIT_BYTES,
    )


def _modulated_norm(x, gain, shift, scale):
    y = x * lax.rsqrt(jnp.mean(x * x, axis=-1, keepdims=True) + EPS)
    return y * (gain * (1.0 + scale)) + shift


def _ff_chunks():
    chunks, start = [], 0
    while start < D_FF:
        size = min(FF_CHUNK, D_FF - start)
        chunks.append((start, size))
        start += size
    return chunks


def _swiglu(h_bf16, w_gu_ref, w_down_ref):
    acc = None
    for start, size in _ff_chunks():
        g = jnp.dot(h_bf16, w_gu_ref[0, :, start:start + size], preferred_element_type=jnp.float32)
        u = jnp.dot(h_bf16, w_gu_ref[0, :, D_FF + start:D_FF + start + size],
                    preferred_element_type=jnp.float32)
        act = (g * jax.nn.sigmoid(g) * u).astype(jnp.bfloat16)
        part = jnp.dot(act, w_down_ref[0, start:start + size, :], preferred_element_type=jnp.float32)
        acc = part if acc is None else acc + part
    return acc


def _adaln_kernel(c_ref, w_ref, b_ref, o_ref):
    c = c_ref[...]
    sc = (c * jax.nn.sigmoid(c)).astype(jnp.bfloat16)
    w = w_ref[0].astype(jnp.bfloat16)
    o_ref[0] = jnp.dot(sc, w, preferred_element_type=jnp.float32) + b_ref[0]


def _adaln_mod(c, w_ada, b_ada):
    depth, _, n_out = w_ada.shape
    batch = c.shape[0]
    n_chunk = n_out // ADALN_COL_CHUNKS
    return pl.pallas_call(
        _adaln_kernel,
        out_shape=jax.ShapeDtypeStruct((depth, batch, n_out), jnp.float32),
        grid=(depth, n_out // n_chunk),
        in_specs=[
            pl.BlockSpec((batch, D_MODEL), lambda i, j: (0, 0)),
            pl.BlockSpec((1, D_MODEL, n_chunk), lambda i, j: (i, 0, j)),
            pl.BlockSpec((1, 1, n_chunk), lambda i, j: (i, 0, j)),
        ],
        out_specs=pl.BlockSpec((1, batch, n_chunk), lambda i, j: (i, 0, j)),
        compiler_params=_params(2),
        name="adaln_mod",
    )(c, w_ada, b_ada.reshape(depth, 1, n_out))


def _attn_pre_kernel(x_ref, mod_ref, gain_ref, w_ref, qg_ref, kg_ref,
                     qt_ref, k_ref, vt_ref, kmean_ref, *, tm):
    mod = mod_ref[0, 0]
    h = _modulated_norm(x_ref[0], gain_ref[...], mod[0:1], mod[1:2]).astype(jnp.bfloat16)
    n_blk = tm // BLOCK
    heads_per_chunk = MXU_COLS // HEAD_DIM
    for chunk in range(3 * D_MODEL // MXU_COLS):
        r = jnp.dot(h, w_ref[:, chunk * MXU_COLS:(chunk + 1) * MXU_COLS],
                    preferred_element_type=jnp.float32)
        kind = (chunk * heads_per_chunk) // N_HEADS
        for half in range(heads_per_chunk):
            head = (chunk * heads_per_chunk) % N_HEADS + half
            t = r[:, half * HEAD_DIM:(half + 1) * HEAD_DIM]
            if kind == 2:
                for blk in range(n_blk):
                    vt_ref[0, head, blk] = t[blk * BLOCK:(blk + 1) * BLOCK].T.astype(jnp.bfloat16)
                continue
            tn = t * lax.rsqrt(jnp.mean(t * t, axis=-1, keepdims=True) + EPS)
            if kind == 0:
                qn = (tn * qg_ref[...]) * Q_SCALE
                for blk in range(n_blk):
                    qt_ref[0, head, blk] = qn[blk * BLOCK:(blk + 1) * BLOCK].T.astype(jnp.bfloat16)
            else:
                kn = tn * kg_ref[...]
                k_ref[0, head] = kn.astype(jnp.bfloat16)
                kmean_ref[0, 0, head] = jnp.mean(kn.reshape(n_blk, BLOCK, HEAD_DIM), axis=1)


def _attn_pre(x, mod, gain, w_qkv_bf16, q_gain, k_gain, *, tm):
    batch, seq, _ = x.shape
    n_tiles = seq // tm
    n_blk = tm // BLOCK
    t_shape = jax.ShapeDtypeStruct((batch, N_HEADS, seq // BLOCK, HEAD_DIM, BLOCK), jnp.bfloat16)
    t_spec = pl.BlockSpec((1, N_HEADS, n_blk, HEAD_DIM, BLOCK), lambda b, t: (b, 0, t, 0, 0))
    return pl.pallas_call(
        functools.partial(_attn_pre_kernel, tm=tm),
        out_shape=(
            t_shape,
            jax.ShapeDtypeStruct((batch, N_HEADS, seq, HEAD_DIM), jnp.bfloat16),
            t_shape,
            jax.ShapeDtypeStruct((batch, n_tiles, N_HEADS, n_blk, HEAD_DIM), jnp.float32),
        ),
        grid=(batch, n_tiles),
        in_specs=[
            pl.BlockSpec((1, tm, D_MODEL), lambda b, t: (b, t, 0)),
            pl.BlockSpec((1, 1, 6, D_MODEL), lambda b, t: (0, b, 0, 0)),
            _const_spec((1, D_MODEL)),
            _const_spec((D_MODEL, 3 * D_MODEL)),
            _const_spec((1, HEAD_DIM)),
            _const_spec((1, HEAD_DIM)),
        ],
        out_specs=(
            t_spec,
            pl.BlockSpec((1, N_HEADS, tm, HEAD_DIM), lambda b, t: (b, 0, t, 0)),
            t_spec,
            pl.BlockSpec((1, 1, N_HEADS, n_blk, HEAD_DIM), lambda b, t: (b, t, 0, 0, 0)),
        ),
        compiler_params=_params(2),
        name="attn_pre",
    )(x, mod, gain, w_qkv_bf16, q_gain, k_gain)


def _moba_kernel(*refs, n_blocks, fixed_shift):
    for half in range(QUERY_BLOCKS_PER_STEP):
        _attend_query_block(pl.program_id(1) * QUERY_BLOCKS_PER_STEP + half, half, *refs,
                            n_blocks=n_blocks, fixed_shift=fixed_shift)


def _attend_query_block(nb, half, qt_ref, k_blk_ref, vt_blk_ref, kmean_ref, shift_ref, o_ref,
                        k_ref, vt_ref, q2t_ref, stage_ref, acc_ref, *stat_refs, n_blocks, fixed_shift):
    for h in range(N_HEADS):
        k_ref[h, nb] = k_blk_ref[0, h, half]
        vt_ref[h, nb] = vt_blk_ref[0, h, half]
    blk_id = lax.broadcasted_iota(jnp.int32, (n_blocks, BLOCK), 0).astype(jnp.float32)
    key_pos = lax.broadcasted_iota(jnp.int32, (BLOCK, BLOCK), 0)
    qry_pos = lax.broadcasted_iota(jnp.int32, (BLOCK, BLOCK), 1)
    lane_id = lax.broadcasted_iota(jnp.int32, (BLOCK, LANES), 1)
    neg_inf = jnp.float32(-jnp.inf)
    zero_rows = jnp.zeros((BLOCK - HEAD_DIM - n_blocks - SHIFT_ROWS, BLOCK), jnp.bfloat16)
    ones_rows = jnp.ones((DENOM_ROWS, BLOCK), jnp.bfloat16)
    if not fixed_shift:
        mb_ref, m_ref = stat_refs

    def route_scores(h):
        kmean = kmean_ref[0, h]
        km_hi = kmean.astype(jnp.bfloat16)
        km_lo = (kmean - km_hi.astype(jnp.float32)).astype(jnp.bfloat16)
        r2 = jnp.dot(jnp.concatenate([km_hi, km_lo], axis=0), qt_ref[0, h, half],
                     preferred_element_type=jnp.float32)
        return r2[:n_blocks] + r2[n_blocks:]

    def route_mask(score):
        score = jnp.where(blk_id < nb.astype(jnp.float32), score, neg_inf)
        mask = jnp.full((n_blocks, BLOCK), MASKED, jnp.float32)
        for _ in range(TOPK_BLOCKS):
            best = jnp.max(score, axis=0, keepdims=True)
            first = jnp.min(jnp.where(score == best, blk_id, float(n_blocks)), axis=0, keepdims=True)
            pick = blk_id == first
            mask = jnp.where(pick & (best > neg_inf), 0.0, mask)
            score = jnp.where(pick, neg_inf, score)
        return mask

    def key_extension(j):
        hit = None if j is None else lane_id == j
        if fixed_shift:
            hit = lane_id == n_blocks if hit is None else hit | (lane_id == n_blocks)
        if hit is None:
            return jnp.zeros((BLOCK, LANES), jnp.bfloat16)
        return jnp.where(hit, 1.0, 0.0).astype(jnp.bfloat16)

    def produce(h, j, extension, slot, own):
        keys = jnp.concatenate([k_ref[h, j], extension], axis=1)
        s = jnp.dot(keys, q2t_ref[h], preferred_element_type=jnp.float32)
        if own:
            s = jnp.where(key_pos <= qry_pos, s, MASKED)
        if fixed_shift:
            stage_ref[slot, h] = jnp.exp2(s).astype(jnp.bfloat16)
        else:
            stage_ref[slot, h] = s
            mb_ref[slot, h] = jnp.max(s, axis=0, keepdims=True)

    def consume(h, j, slot, first):
        values = jnp.concatenate([vt_ref[h, j], ones_rows], axis=0)
        if fixed_shift:
            pv = jnp.dot(values, stage_ref[slot, h], preferred_element_type=jnp.float32)
            acc_ref[h] = pv if first else acc_ref[h] + pv
            return
        m_blk = mb_ref[slot, h]
        if first:
            m_new = m_blk
        else:
            m = m_ref[h]
            m_new = jnp.maximum(m, m_blk)
            alpha = jnp.exp2(m - m_new)
        p = jnp.exp2(stage_ref[slot, h] - m_new).astype(jnp.bfloat16)
        pv = jnp.dot(values, p, preferred_element_type=jnp.float32)
        m_ref[h] = m_new
        acc_ref[h] = pv if first else alpha * acc_ref[h] + pv

    def run_blocks(first_block, count):
        for t in range(count):
            j = first_block + t
            j_next = jnp.minimum(j + 1, n_blocks - 1)
            extension = key_extension(j_next)
            for h in range(N_HEADS):
                produce(h, j_next, extension, (t + 1) % 2, False)
                consume(h, j, t % 2, False)

    unrouted = jnp.concatenate([jnp.zeros((n_blocks, BLOCK), jnp.float32), shift_ref[...]],
                               axis=0).astype(jnp.bfloat16)
    for h in range(N_HEADS):
        q2t_ref[h] = jnp.concatenate([qt_ref[0, h, half], unrouted, zero_rows], axis=0)
    routing_scores = [route_scores(h) for h in range(N_HEADS)]
    extension = key_extension(None)
    for h in range(N_HEADS):
        produce(h, nb, extension, 1, True)
    for h in range(N_HEADS):
        q2t_ref[h, HEAD_DIM:HEAD_DIM + n_blocks, :] = route_mask(routing_scores[h]).astype(jnp.bfloat16)
    extension = key_extension(0)
    for h in range(N_HEADS):
        produce(h, 0, extension, 0, False)
        consume(h, nb, 1, True)

    def block_group(i, _):
        run_blocks(LOOP_BLOCKS * i, LOOP_BLOCKS)
        return 0

    n_groups = lax.shift_right_logical(nb, LOOP_BLOCKS.bit_length() - 1)
    lax.fori_loop(0, n_groups, block_group, 0)
    done = n_groups * LOOP_BLOCKS
    count = LOOP_BLOCKS // 2
    while count:
        @pl.when(jnp.bitwise_and(nb, count) != 0)
        def _(first_block=done, count=count):
            run_blocks(first_block, count)
        done = done + jnp.bitwise_and(nb, count)
        count //= 2
    for h in range(N_HEADS):
        out_t = acc_ref[h, 0:HEAD_DIM, :] / acc_ref[h, HEAD_DIM:HEAD_DIM + 1, :]
        o_ref[0, half * BLOCK:(half + 1) * BLOCK, h * HEAD_DIM:(h + 1) * HEAD_DIM] = (
            out_t.T.astype(o_ref.dtype))


def _moba_attention(qt, k, vt, kmean, shift_rows, *, fixed_shift):
    batch, n_heads, n_blocks, _, _ = qt.shape
    seq = n_blocks * BLOCK
    k5 = k.reshape(batch, n_heads, n_blocks, BLOCK, HEAD_DIM)
    per_step = QUERY_BLOCKS_PER_STEP
    own_blocks = lambda b, g: (b, 0, g, 0, 0)
    stage_dtype = jnp.bfloat16 if fixed_shift else jnp.float32
    scratch = [
        pltpu.VMEM((n_heads, n_blocks, BLOCK, HEAD_DIM), jnp.bfloat16),
        pltpu.VMEM((n_heads, n_blocks, HEAD_DIM, BLOCK), jnp.bfloat16),
        pltpu.VMEM((n_heads, BLOCK, BLOCK), jnp.bfloat16),
        pltpu.VMEM((2, n_heads, BLOCK, BLOCK), stage_dtype),
        pltpu.VMEM((n_heads, HEAD_DIM + DENOM_ROWS, BLOCK), jnp.float32),
    ]
    if not fixed_shift:
        scratch += [
            pltpu.VMEM((2, n_heads, 1, BLOCK), jnp.float32),
            pltpu.VMEM((n_heads, 1, BLOCK), jnp.float32),
        ]
    return pl.pallas_call(
        functools.partial(_moba_kernel, n_blocks=n_blocks, fixed_shift=fixed_shift),
        out_shape=jax.ShapeDtypeStruct((batch, seq, D_MODEL), jnp.bfloat16),
        grid=(batch, n_blocks // per_step),
        in_specs=[
            pl.BlockSpec((1, n_heads, per_step, HEAD_DIM, BLOCK), own_blocks),
            pl.BlockSpec((1, n_heads, per_step, BLOCK, HEAD_DIM), own_blocks),
            pl.BlockSpec((1, n_heads, per_step, HEAD_DIM, BLOCK), own_blocks),
            pl.BlockSpec((1, n_heads, n_blocks, HEAD_DIM), lambda b, g: (b, 0, 0, 0)),
            _const_spec(shift_rows.shape),
        ],
        out_specs=pl.BlockSpec((1, per_step * BLOCK, D_MODEL), lambda b, g: (b, g, 0)),
        scratch_shapes=scratch,
        compiler_params=_params(2),
        name="moba_attn_fixed_shift" if fixed_shift else "moba_attn_online",
    )(qt, k5, vt, kmean, shift_rows)


def _attn_post_kernel(x_ref, o_ref, mod_ref, gain_ref, w_o_ref, w_gu_ref, w_down_ref, out_ref):
    mod = mod_ref[0, 0]
    y = jnp.dot(o_ref[0], w_o_ref[...], preferred_element_type=jnp.float32)
    x = x_ref[0] + mod[2:3] * y
    h = _modulated_norm(x, gain_ref[...], mod[3:4], mod[4:5]).astype(jnp.bfloat16)
    out_ref[0] = x + mod[5:6] * _swiglu(h, w_gu_ref, w_down_ref)


def _attn_post(x, o, mod, gain_ffn, w_o, w_gu, w_down, *, tm):
    batch, seq, _ = x.shape
    tile = lambda b, t: (b, t, 0)
    return pl.pallas_call(
        _attn_post_kernel,
        out_shape=jax.ShapeDtypeStruct(x.shape, x.dtype),
        grid=(batch, seq // tm),
        in_specs=[
            pl.BlockSpec((1, tm, D_MODEL), tile),
            pl.BlockSpec((1, tm, D_MODEL), tile),
            pl.BlockSpec((1, 1, 6, D_MODEL), lambda b, t: (0, b, 0, 0)),
            _const_spec((1, D_MODEL)),
            _const_spec(w_o.shape),
            _layer_spec(w_gu.shape, 0),
            _layer_spec(w_down.shape, 0),
        ],
        out_specs=pl.BlockSpec((1, tm, D_MODEL), tile),
        compiler_params=_params(2),
        name="attn_post",
    )(x, o, mod, gain_ffn, w_o, w_gu, w_down)


def _conv_layer_kernel(x_ref, mod_ref, gain_mix_ref, gain_ffn_ref, w_in_ref, conv_w_ref,
                       w_out_ref, w_gu_ref, w_down_ref, out_ref, u_ref, *, tm):
    mod = mod_ref[0, 0]
    h = _modulated_norm(x_ref[0], gain_mix_ref[...], mod[0:1], mod[1:2]).astype(jnp.bfloat16)

    @pl.when(pl.program_id(1) == 0)
    def _():
        u_ref[0:SUBLANES, :] = jnp.zeros((SUBLANES, D_MODEL), jnp.float32)

    @pl.when(pl.program_id(1) > 0)
    def _():
        u_ref[0:SUBLANES, :] = u_ref[tm:tm + SUBLANES, :]

    b_gate = jnp.dot(h, w_in_ref[:, :D_MODEL], preferred_element_type=jnp.float32)
    c_gate = jnp.dot(h, w_in_ref[:, D_MODEL:2 * D_MODEL], preferred_element_type=jnp.float32)
    u = c_gate * jnp.dot(h, w_in_ref[:, 2 * D_MODEL:], preferred_element_type=jnp.float32)
    u_ref[SUBLANES:, :] = u
    y = conv_w_ref[CONV_WIDTH - 1:CONV_WIDTH, :] * u
    for tap in range(CONV_WIDTH - 1):
        back = CONV_WIDTH - 1 - tap
        y = y + conv_w_ref[tap:tap + 1, :] * u_ref[SUBLANES - back:SUBLANES - back + tm, :]
    mix = jnp.dot((b_gate * y).astype(jnp.bfloat16), w_out_ref[...], preferred_element_type=jnp.float32)
    x = x_ref[0] + mod[2:3] * mix
    h = _modulated_norm(x, gain_ffn_ref[...], mod[3:4], mod[4:5]).astype(jnp.bfloat16)
    out_ref[0] = x + mod[5:6] * _swiglu(h, w_gu_ref, w_down_ref)


def _conv_layer(x, mod, gain_mix, gain_ffn, w_in, conv_w, w_out, w_gu, w_down, *, tm):
    batch, seq, _ = x.shape
    tile = lambda b, t: (b, t, 0)
    return pl.pallas_call(
        functools.partial(_conv_layer_kernel, tm=tm),
        out_shape=jax.ShapeDtypeStruct(x.shape, x.dtype),
        grid=(batch, seq // tm),
        in_specs=[
            pl.BlockSpec((1, tm, D_MODEL), tile),
            pl.BlockSpec((1, 1, 6, D_MODEL), lambda b, t: (1, b, 0, 0)),
            _const_spec((1, D_MODEL)),
            _const_spec((1, D_MODEL)),
            _const_spec(w_in.shape),
            _const_spec(conv_w.shape),
            _const_spec(w_out.shape),
            _layer_spec(w_gu.shape, 1),
            _layer_spec(w_down.shape, 1),
        ],
        out_specs=pl.BlockSpec((1, tm, D_MODEL), tile),
        scratch_shapes=[pltpu.VMEM((tm + SUBLANES, D_MODEL), jnp.float32)],
        compiler_params=_params(2),
        name="conv_layer",
    )(x, mod, gain_mix, gain_ffn, w_in, conv_w, w_out, w_gu, w_down)


def kernel(x, c, w_ada, b_ada, norm_mix, norm_ffn, w_qkv, w_o, q_gain, k_gain,
           w_in, conv_w, w_out, w_gate_up, w_down):
    batch, seq, _ = x.shape
    depth = w_ada.shape[0]
    assert depth == 2 and seq % BLOCK == 0
    bf16 = jnp.bfloat16
    mod = _adaln_mod(c, w_ada, b_ada).reshape(depth, batch, 6, D_MODEL)

    tm_pre, tm_post, tm_conv = 1024, 1024, 1024
    q, k, vt, kmean = _attn_pre(x, mod, norm_mix[0:1], w_qkv[0].astype(bf16), q_gain[0:1], k_gain[0:1],
                                tm=tm_pre)
    kmean = kmean.transpose(0, 2, 1, 3, 4).reshape(batch, N_HEADS, seq // BLOCK, HEAD_DIM)
    score_bound = HEAD_DIM * Q_SCALE * jnp.max(jnp.abs(q_gain[0])) * jnp.max(jnp.abs(k_gain[0]))
    shift_rows = jnp.zeros((SHIFT_ROWS, BLOCK), jnp.float32).at[0].set(-score_bound)
    o = lax.cond(
        score_bound <= MAX_FIXED_SHIFT,
        functools.partial(_moba_attention, fixed_shift=True),
        functools.partial(_moba_attention, fixed_shift=False),
        q, k, vt, kmean, shift_rows)
    w_gate_up, w_down = w_gate_up.astype(bf16), w_down.astype(bf16)
    x = _attn_post(x, o, mod, norm_ffn[0:1], w_o[0].astype(bf16), w_gate_up, w_down, tm=tm_post)
    x = _conv_layer(x, mod, norm_mix[1:2], norm_ffn[1:2], w_in[0].astype(bf16), conv_w[0],
                    w_out[0].astype(bf16), w_gate_up, w_down, tm=tm_conv)
    return x
```

```python
import functools
import math

import jax
import jax.numpy as jnp
from jax import lax
from jax.experimental import pallas as pl
from jax.experimental.pallas import tpu as pltpu

D_MODEL = 1024
N_HEADS = 8
HEAD_DIM = D_MODEL // N_HEADS
BLOCK = 256
TOPK_BLOCKS = 3
CONV_WIDTH = 3
D_FF = int(math.ceil((8 * D_MODEL / 3) / 256) * 256)
EPS = 1e-6

MXU_COLS = 256
LANES = 128
FF_CHUNK = 2 * MXU_COLS
SUBLANES = 8
DENOM_ROWS = 16
SHIFT_ROWS = 16
LOOP_BLOCKS = 8
MAX_FIXED_SHIFT = 40.0
VMEM_LIMIT_BYTES = 56 * 1024 * 1024

MASKED = -1e30
Q_SCALE = HEAD_DIM ** -0.5 * math.log2(math.e)


def _const_spec(shape):
    zeros = (0,) * len(shape)
    return pl.BlockSpec(shape, lambda *_: zeros, pipeline_mode=pl.Buffered(1))


def _layer_spec(stacked_shape, layer):
    tail = (0,) * (len(stacked_shape) - 1)
    return pl.BlockSpec((1,) + tuple(stacked_shape[1:]), lambda *_: (layer,) + tail,
                        pipeline_mode=pl.Buffered(1))


def _params(n_grid_axes, flags=None):
    return pltpu.CompilerParams(
        dimension_semantics=("arbitrary",) * n_grid_axes,
        vmem_limit_bytes=VMEM_LIMIT_BYTES,
        flags=flags,
    )


def _modulated_norm(x, gain, shift, scale):
    y = x * lax.rsqrt(jnp.mean(x * x, axis=-1, keepdims=True) + EPS)
    return y * (gain * (1.0 + scale)) + shift


def _ff_chunks():
    chunks, start = [], 0
    while start < D_FF:
        size = min(FF_CHUNK, D_FF - start)
        chunks.append((start, size))
        start += size
    return chunks


def _swiglu(h_bf16, w_gu_ref, w_down_ref):
    acc = None
    for start, size in _ff_chunks():
        g = jnp.dot(h_bf16, w_gu_ref[0, :, start:start + size], preferred_element_type=jnp.float32)
        u = jnp.dot(h_bf16, w_gu_ref[0, :, D_FF + start:D_FF + start + size],
                    preferred_element_type=jnp.float32)
        act = (g * jax.nn.sigmoid(g) * u).astype(jnp.bfloat16)
        part = jnp.dot(act, w_down_ref[0, start:start + size, :], preferred_element_type=jnp.float32)
        acc = part if acc is None else acc + part
    return acc


def _adaln_kernel(c_ref, w_ref, b_ref, o_ref):
    c = c_ref[...]
    sc = (c * jax.nn.sigmoid(c)).astype(jnp.bfloat16)
    w = w_ref[0].astype(jnp.bfloat16)
    o_ref[0] = jnp.dot(sc, w, preferred_element_type=jnp.float32) + b_ref[0]


def _adaln_mod(c, w_ada, b_ada):
    depth, _, n_out = w_ada.shape
    batch = c.shape[0]
    n_chunk = n_out // 4
    return pl.pallas_call(
        _adaln_kernel,
        out_shape=jax.ShapeDtypeStruct((depth, batch, n_out), jnp.float32),
        grid=(depth, n_out // n_chunk),
        in_specs=[
            pl.BlockSpec((batch, D_MODEL), lambda i, j: (0, 0)),
            pl.BlockSpec((1, D_MODEL, n_chunk), lambda i, j: (i, 0, j)),
            pl.BlockSpec((1, 1, n_chunk), lambda i, j: (i, 0, j)),
        ],
        out_specs=pl.BlockSpec((1, batch, n_chunk), lambda i, j: (i, 0, j)),
        compiler_params=_params(2),
        name="adaln_mod",
    )(c, w_ada, b_ada.reshape(depth, 1, n_out))


def _attn_pre_kernel(x_ref, mod_ref, gain_ref, w_ref, qg_ref, kg_ref,
                     qt_ref, k_ref, vt_ref, kmean_ref, *, tm):
    mod = mod_ref[0, 0]
    h = _modulated_norm(x_ref[0], gain_ref[...], mod[0:1], mod[1:2]).astype(jnp.bfloat16)
    n_blk = tm // BLOCK
    heads_per_chunk = MXU_COLS // HEAD_DIM
    for chunk in range(3 * D_MODEL // MXU_COLS):
        r = jnp.dot(h, w_ref[:, chunk * MXU_COLS:(chunk + 1) * MXU_COLS],
                    preferred_element_type=jnp.float32)
        kind = (chunk * heads_per_chunk) // N_HEADS
        for half in range(heads_per_chunk):
            head = (chunk * heads_per_chunk) % N_HEADS + half
            t = r[:, half * HEAD_DIM:(half + 1) * HEAD_DIM]
            if kind == 2:
                for blk in range(n_blk):
                    vt_ref[0, head, blk] = t[blk * BLOCK:(blk + 1) * BLOCK].T.astype(jnp.bfloat16)
                continue
            tn = t * lax.rsqrt(jnp.mean(t * t, axis=-1, keepdims=True) + EPS)
            if kind == 0:
                qn = (tn * qg_ref[...]) * Q_SCALE
                for blk in range(n_blk):
                    qt_ref[0, head, blk] = qn[blk * BLOCK:(blk + 1) * BLOCK].T.astype(jnp.bfloat16)
            else:
                kn = tn * kg_ref[...]
                k_ref[0, head] = kn.astype(jnp.bfloat16)
                kmean_ref[0, 0, head] = jnp.mean(kn.reshape(n_blk, BLOCK, HEAD_DIM), axis=1)


def _attn_pre(x, mod, gain, w_qkv_bf16, q_gain, k_gain, *, tm):
    batch, seq, _ = x.shape
    n_tiles = seq // tm
    n_blk = tm // BLOCK
    t_shape = jax.ShapeDtypeStruct((batch, N_HEADS, seq // BLOCK, HEAD_DIM, BLOCK), jnp.bfloat16)
    t_spec = pl.BlockSpec((1, N_HEADS, n_blk, HEAD_DIM, BLOCK), lambda b, t: (b, 0, t, 0, 0))
    return pl.pallas_call(
        functools.partial(_attn_pre_kernel, tm=tm),
        out_shape=(
            t_shape,
            jax.ShapeDtypeStruct((batch, N_HEADS, seq, HEAD_DIM), jnp.bfloat16),
            t_shape,
            jax.ShapeDtypeStruct((batch, n_tiles, N_HEADS, n_blk, HEAD_DIM), jnp.float32),
        ),
        grid=(batch, n_tiles),
        in_specs=[
            pl.BlockSpec((1, tm, D_MODEL), lambda b, t: (b, t, 0)),
            pl.BlockSpec((1, 1, 6, D_MODEL), lambda b, t: (0, b, 0, 0)),
            _const_spec((1, D_MODEL)),
            _const_spec((D_MODEL, 3 * D_MODEL)),
            _const_spec((1, HEAD_DIM)),
            _const_spec((1, HEAD_DIM)),
        ],
        out_specs=(
            t_spec,
            pl.BlockSpec((1, N_HEADS, tm, HEAD_DIM), lambda b, t: (b, 0, t, 0)),
            t_spec,
            pl.BlockSpec((1, 1, N_HEADS, n_blk, HEAD_DIM), lambda b, t: (b, t, 0, 0, 0)),
        ),
        compiler_params=_params(2),
        name="attn_pre",
    )(x, mod, gain, w_qkv_bf16, q_gain, k_gain)


def _moba_kernel(qt_ref, k_blk_ref, vt_blk_ref, kmean_ref, shift_ref, o_ref,
                 k_ref, vt_ref, q2t_ref, stage_ref, acc_ref, *stat_refs, n_blocks, fixed_shift):
    nb = pl.program_id(1)
    for h in range(N_HEADS):
        k_ref[h, nb] = k_blk_ref[0, h, 0]
        vt_ref[h, nb] = vt_blk_ref[0, h, 0]
    blk_id = lax.broadcasted_iota(jnp.int32, (n_blocks, BLOCK), 0).astype(jnp.float32)
    key_pos = lax.broadcasted_iota(jnp.int32, (BLOCK, BLOCK), 0)
    qry_pos = lax.broadcasted_iota(jnp.int32, (BLOCK, BLOCK), 1)
    lane_id = lax.broadcasted_iota(jnp.int32, (BLOCK, LANES), 1)
    neg_inf = jnp.float32(-jnp.inf)
    zero_rows = jnp.zeros((BLOCK - HEAD_DIM - n_blocks - SHIFT_ROWS, BLOCK), jnp.bfloat16)
    ones_rows = jnp.ones((DENOM_ROWS, BLOCK), jnp.bfloat16)
    if not fixed_shift:
        mb_ref, m_ref = stat_refs

    def route_scores(h):
        kmean = kmean_ref[0, h]
        km_hi = kmean.astype(jnp.bfloat16)
        km_lo = (kmean - km_hi.astype(jnp.float32)).astype(jnp.bfloat16)
        r2 = jnp.dot(jnp.concatenate([km_hi, km_lo], axis=0), qt_ref[0, h, 0],
                     preferred_element_type=jnp.float32)
        return r2[:n_blocks] + r2[n_blocks:]

    def route_mask(score):
        score = jnp.where(blk_id < nb.astype(jnp.float32), score, neg_inf)
        mask = jnp.full((n_blocks, BLOCK), MASKED, jnp.float32)
        for _ in range(TOPK_BLOCKS):
            best = jnp.max(score, axis=0, keepdims=True)
            first = jnp.min(jnp.where(score == best, blk_id, float(n_blocks)), axis=0, keepdims=True)
            pick = blk_id == first
            mask = jnp.where(pick & (best > neg_inf), 0.0, mask)
            score = jnp.where(pick, neg_inf, score)
        return mask

    def key_extension(j):
        hit = None if j is None else lane_id == j
        if fixed_shift:
            hit = lane_id == n_blocks if hit is None else hit | (lane_id == n_blocks)
        if hit is None:
            return jnp.zeros((BLOCK, LANES), jnp.bfloat16)
        return jnp.where(hit, 1.0, 0.0).astype(jnp.bfloat16)

    def produce(h, j, extension, slot, own):
        keys = jnp.concatenate([k_ref[h, j], extension], axis=1)
        s = jnp.dot(keys, q2t_ref[h], preferred_element_type=jnp.float32)
        if own:
            s = jnp.where(key_pos <= qry_pos, s, MASKED)
        if fixed_shift:
            stage_ref[slot, h] = jnp.exp2(s).astype(jnp.bfloat16)
        else:
            stage_ref[slot, h] = s
            mb_ref[slot, h] = jnp.max(s, axis=0, keepdims=True)

    def consume(h, j, slot, first):
        values = jnp.concatenate([vt_ref[h, j], ones_rows], axis=0)
        if fixed_shift:
            pv = jnp.dot(values, stage_ref[slot, h], preferred_element_type=jnp.float32)
            acc_ref[h] = pv if first else acc_ref[h] + pv
            return
        m_blk = mb_ref[slot, h]
        if first:
            m_new = m_blk
        else:
            m = m_ref[h]
            m_new = jnp.maximum(m, m_blk)
            alpha = jnp.exp2(m - m_new)
        p = jnp.exp2(stage_ref[slot, h] - m_new).astype(jnp.bfloat16)
        pv = jnp.dot(values, p, preferred_element_type=jnp.float32)
        m_ref[h] = m_new
        acc_ref[h] = pv if first else alpha * acc_ref[h] + pv

    def run_blocks(first_block, count, last):
        for t in range(count):
            j = first_block + t
            look_ahead = not (last and t == count - 1)
            if look_ahead:
                extension = key_extension(j + 1)
            for h in range(N_HEADS):
                if look_ahead:
                    produce(h, j + 1, extension, (t + 1) % 2, False)
                consume(h, j, t % 2, False)

    unrouted = jnp.concatenate([jnp.zeros((n_blocks, BLOCK), jnp.float32), shift_ref[...]],
                               axis=0).astype(jnp.bfloat16)
    for h in range(N_HEADS):
        q2t_ref[h] = jnp.concatenate([qt_ref[0, h, 0], unrouted, zero_rows], axis=0)
    routing_scores = [route_scores(h) for h in range(N_HEADS)]
    extension = key_extension(None)
    for h in range(N_HEADS):
        produce(h, nb, extension, 1, True)
    for h in range(N_HEADS):
        q2t_ref[h, HEAD_DIM:HEAD_DIM + n_blocks, :] = route_mask(routing_scores[h]).astype(jnp.bfloat16)
    extension = key_extension(0)
    for h in range(N_HEADS):
        produce(h, 0, extension, 0, False)
        consume(h, nb, 1, True)

    def block_group(i, _):
        run_blocks(LOOP_BLOCKS * i, LOOP_BLOCKS, False)
        return 0

    n_groups = lax.shift_right_logical(nb, LOOP_BLOCKS.bit_length() - 1)
    group_is_last = jnp.logical_and(jnp.bitwise_and(nb, LOOP_BLOCKS - 1) == 0, n_groups > 0)
    n_plain_groups = n_groups - group_is_last.astype(jnp.int32)
    lax.fori_loop(0, n_plain_groups, block_group, 0)

    @pl.when(group_is_last)
    def _():
        run_blocks(LOOP_BLOCKS * n_plain_groups, LOOP_BLOCKS, True)

    done = n_groups * LOOP_BLOCKS
    count = LOOP_BLOCKS // 2
    while count:
        present = jnp.bitwise_and(nb, count) != 0
        nothing_smaller = jnp.bitwise_and(nb, count - 1) == 0
        for last in ((True, False) if count > 1 else (True,)):
            ends_here = nothing_smaller if last else jnp.logical_not(nothing_smaller)

            @pl.when(jnp.logical_and(present, ends_here))
            def _(first_block=done, count=count, last=last):
                run_blocks(first_block, count, last)
        done = done + jnp.bitwise_and(nb, count)
        count //= 2
    for h in range(N_HEADS):
        out_t = acc_ref[h, 0:HEAD_DIM, :] / acc_ref[h, HEAD_DIM:HEAD_DIM + 1, :]
        o_ref[0, :, h * HEAD_DIM:(h + 1) * HEAD_DIM] = out_t.T.astype(o_ref.dtype)


def _moba_attention(qt, k, vt, kmean, shift_rows, *, fixed_shift):
    batch, n_heads, n_blocks, _, _ = qt.shape
    seq = n_blocks * BLOCK
    k5 = k.reshape(batch, n_heads, n_blocks, BLOCK, HEAD_DIM)
    own_block = lambda b, nb: (b, 0, nb, 0, 0)
    stage_dtype = jnp.bfloat16 if fixed_shift else jnp.float32
    scratch = [
        pltpu.VMEM((n_heads, n_blocks, BLOCK, HEAD_DIM), jnp.bfloat16),
        pltpu.VMEM((n_heads, n_blocks, HEAD_DIM, BLOCK), jnp.bfloat16),
        pltpu.VMEM((n_heads, BLOCK, BLOCK), jnp.bfloat16),
        pltpu.VMEM((2, n_heads, BLOCK, BLOCK), stage_dtype),
        pltpu.VMEM((n_heads, HEAD_DIM + DENOM_ROWS, BLOCK), jnp.float32),
    ]
    if not fixed_shift:
        scratch += [
            pltpu.VMEM((2, n_heads, 1, BLOCK), jnp.float32),
            pltpu.VMEM((n_heads, 1, BLOCK), jnp.float32),
        ]
    return pl.pallas_call(
        functools.partial(_moba_kernel, n_blocks=n_blocks, fixed_shift=fixed_shift),
        out_shape=jax.ShapeDtypeStruct((batch, seq, D_MODEL), jnp.bfloat16),
        grid=(batch, n_blocks),
        in_specs=[
            pl.BlockSpec((1, n_heads, 1, HEAD_DIM, BLOCK), own_block),
            pl.BlockSpec((1, n_heads, 1, BLOCK, HEAD_DIM), own_block),
            pl.BlockSpec((1, n_heads, 1, HEAD_DIM, BLOCK), own_block),
            pl.BlockSpec((1, n_heads, n_blocks, HEAD_DIM), lambda b, nb: (b, 0, 0, 0)),
            _const_spec(shift_rows.shape),
        ],
        out_specs=pl.BlockSpec((1, BLOCK, D_MODEL), lambda b, nb: (b, nb, 0)),
        scratch_shapes=scratch,
        compiler_params=_params(2),
        name="moba_attn_fixed_shift" if fixed_shift else "moba_attn_online",
    )(qt, k5, vt, kmean, shift_rows)


def _attn_post_kernel(x_ref, o_ref, mod_ref, gain_ref, w_o_ref, w_gu_ref, w_down_ref, out_ref):
    mod = mod_ref[0, 0]
    y = jnp.dot(o_ref[0], w_o_ref[...], preferred_element_type=jnp.float32)
    x = x_ref[0] + mod[2:3] * y
    h = _modulated_norm(x, gain_ref[...], mod[3:4], mod[4:5]).astype(jnp.bfloat16)
    out_ref[0] = x + mod[5:6] * _swiglu(h, w_gu_ref, w_down_ref)


def _attn_post(x, o, mod, gain_ffn, w_o, w_gu, w_down, *, tm):
    batch, seq, _ = x.shape
    tile = lambda b, t: (b, t, 0)
    return pl.pallas_call(
        _attn_post_kernel,
        out_shape=jax.ShapeDtypeStruct(x.shape, x.dtype),
        grid=(batch, seq // tm),
        in_specs=[
            pl.BlockSpec((1, tm, D_MODEL), tile),
            pl.BlockSpec((1, tm, D_MODEL), tile),
            pl.BlockSpec((1, 1, 6, D_MODEL), lambda b, t: (0, b, 0, 0)),
            _const_spec((1, D_MODEL)),
            _const_spec(w_o.shape),
            _layer_spec(w_gu.shape, 0),
            _layer_spec(w_down.shape, 0),
        ],
        out_specs=pl.BlockSpec((1, tm, D_MODEL), tile),
        compiler_params=_params(2),
        name="attn_post",
    )(x, o, mod, gain_ffn, w_o, w_gu, w_down)


def _conv_layer_kernel(x_ref, mod_ref, gain_mix_ref, gain_ffn_ref, w_in_ref, conv_w_ref,
                       w_out_ref, w_gu_ref, w_down_ref, out_ref, u_ref, *, tm):
    mod = mod_ref[0, 0]
    h = _modulated_norm(x_ref[0], gain_mix_ref[...], mod[0:1], mod[1:2]).astype(jnp.bfloat16)

    @pl.when(pl.program_id(1) == 0)
    def _():
        u_ref[0:SUBLANES, :] = jnp.zeros((SUBLANES, D_MODEL), jnp.float32)

    @pl.when(pl.program_id(1) > 0)
    def _():
        u_ref[0:SUBLANES, :] = u_ref[tm:tm + SUBLANES, :]

    b_gate = jnp.dot(h, w_in_ref[:, :D_MODEL], preferred_element_type=jnp.float32)
    c_gate = jnp.dot(h, w_in_ref[:, D_MODEL:2 * D_MODEL], preferred_element_type=jnp.float32)
    u = c_gate * jnp.dot(h, w_in_ref[:, 2 * D_MODEL:], preferred_element_type=jnp.float32)
    u_ref[SUBLANES:, :] = u
    y = conv_w_ref[CONV_WIDTH - 1:CONV_WIDTH, :] * u
    for tap in range(CONV_WIDTH - 1):
        back = CONV_WIDTH - 1 - tap
        y = y + conv_w_ref[tap:tap + 1, :] * u_ref[SUBLANES - back:SUBLANES - back + tm, :]
    mix = jnp.dot((b_gate * y).astype(jnp.bfloat16), w_out_ref[...], preferred_element_type=jnp.float32)
    x = x_ref[0] + mod[2:3] * mix
    h = _modulated_norm(x, gain_ffn_ref[...], mod[3:4], mod[4:5]).astype(jnp.bfloat16)
    out_ref[0] = x + mod[5:6] * _swiglu(h, w_gu_ref, w_down_ref)


def _conv_layer(x, mod, gain_mix, gain_ffn, w_in, conv_w, w_out, w_gu, w_down, *, tm):
    batch, seq, _ = x.shape
    tile = lambda b, t: (b, t, 0)
    return pl.pallas_call(
        functools.partial(_conv_layer_kernel, tm=tm),
        out_shape=jax.ShapeDtypeStruct(x.shape, x.dtype),
        grid=(batch, seq // tm),
        in_specs=[
            pl.BlockSpec((1, tm, D_MODEL), tile),
            pl.BlockSpec((1, 1, 6, D_MODEL), lambda b, t: (1, b, 0, 0)),
            _const_spec((1, D_MODEL)),
            _const_spec((1, D_MODEL)),
            _const_spec(w_in.shape),
            _const_spec(conv_w.shape),
            _const_spec(w_out.shape),
            _layer_spec(w_gu.shape, 1),
            _layer_spec(w_down.shape, 1),
        ],
        out_specs=pl.BlockSpec((1, tm, D_MODEL), tile),
        scratch_shapes=[pltpu.VMEM((tm + SUBLANES, D_MODEL), jnp.float32)],
        compiler_params=_params(2),
        name="conv_layer",
    )(x, mod, gain_mix, gain_ffn, w_in, conv_w, w_out, w_gu, w_down)


def kernel(x, c, w_ada, b_ada, norm_mix, norm_ffn, w_qkv, w_o, q_gain, k_gain,
           w_in, conv_w, w_out, w_gate_up, w_down):
    batch, seq, _ = x.shape
    depth = w_ada.shape[0]
    assert depth == 2 and seq % BLOCK == 0
    bf16 = jnp.bfloat16
    mod = _adaln_mod(c, w_ada, b_ada).reshape(depth, batch, 6, D_MODEL)

    tm_pre, tm_post, tm_conv = 1024, 1024, 512
    q, k, vt, kmean = _attn_pre(x, mod, norm_mix[0:1], w_qkv[0].astype(bf16), q_gain[0:1], k_gain[0:1],
                                tm=tm_pre)
    kmean = kmean.transpose(0, 2, 1, 3, 4).reshape(batch, N_HEADS, seq // BLOCK, HEAD_DIM)
    score_bound = HEAD_DIM * Q_SCALE * jnp.max(jnp.abs(q_gain[0])) * jnp.max(jnp.abs(k_gain[0]))
    shift_rows = jnp.zeros((SHIFT_ROWS, BLOCK), jnp.float32).at[0].set(-score_bound)
    o = lax.cond(
        score_bound <= MAX_FIXED_SHIFT,
        functools.partial(_moba_attention, fixed_shift=True),
        functools.partial(_moba_attention, fixed_shift=False),
        q, k, vt, kmean, shift_rows)
    w_gate_up, w_down = w_gate_up.astype(bf16), w_down.astype(bf16)
    x = _attn_post(x, o, mod, norm_ffn[0:1], w_o[0].astype(bf16), w_gate_up, w_down, tm=tm_post)
    x = _conv_layer(x, mod, norm_mix[1:2], norm_ffn[1:2], w_in[0].astype(bf16), conv_w[0],
                    w_out[0].astype(bf16), w_gate_up, w_down, tm=tm_conv)
    return x
```

```python
import functools
import math

import jax
import jax.numpy as jnp
from jax import lax
from jax.experimental import pallas as pl
from jax.experimental.pallas import tpu as pltpu

D_MODEL = 1024
N_HEADS = 8
HEAD_DIM = D_MODEL // N_HEADS
BLOCK = 256
TOPK_BLOCKS = 3
CONV_WIDTH = 3
D_FF = int(math.ceil((8 * D_MODEL / 3) / 256) * 256)
EPS = 1e-6

MXU_COLS = 256
LANES = 128
FF_CHUNK = 2 * MXU_COLS
SUBLANES = 8
DENOM_ROWS = 16
SHIFT_ROWS = 16
LOOP_BLOCKS = 8
ADALN_COL_CHUNKS = 4
MAX_FIXED_SHIFT = 40.0
VMEM_LIMIT_BYTES = 56 * 1024 * 1024

MASKED = -1e30
Q_SCALE = HEAD_DIM ** -0.5 * math.log2(math.e)


def _const_spec(shape):
    zeros = (0,) * len(shape)
    return pl.BlockSpec(shape, lambda *_: zeros, pipeline_mode=pl.Buffered(1))


def _layer_spec(stacked_shape, layer):
    tail = (0,) * (len(stacked_shape) - 1)
    return pl.BlockSpec((1,) + tuple(stacked_shape[1:]), lambda *_: (layer,) + tail,
                        pipeline_mode=pl.Buffered(1))


def _params(n_grid_axes):
    return pltpu.CompilerParams(
        dimension_semantics=("arbitrary",) * n_grid_axes,
        vmem_limit_bytes=VMEM_LIMIT_BYTES,
    )


def _modulated_norm(x, gain, shift, scale):
    y = x * lax.rsqrt(jnp.mean(x * x, axis=-1, keepdims=True) + EPS)
    return y * (gain * (1.0 + scale)) + shift


def _ff_chunks():
    chunks, start = [], 0
    while start < D_FF:
        size = min(FF_CHUNK, D_FF - start)
        chunks.append((start, size))
        start += size
    return chunks


def _swiglu(h_bf16, w_gu_ref, w_down_ref):
    acc = None
    for start, size in _ff_chunks():
        g = jnp.dot(h_bf16, w_gu_ref[0, :, start:start + size], preferred_element_type=jnp.float32)
        u = jnp.dot(h_bf16, w_gu_ref[0, :, D_FF + start:D_FF + start + size],
                    preferred_element_type=jnp.float32)
        act = (g * jax.nn.sigmoid(g) * u).astype(jnp.bfloat16)
        part = jnp.dot(act, w_down_ref[0, start:start + size, :], preferred_element_type=jnp.float32)
        acc = part if acc is None else acc + part
    return acc


def _adaln_kernel(c_ref, w_ref, b_ref, o_ref):
    c = c_ref[...]
    sc = (c * jax.nn.sigmoid(c)).astype(jnp.bfloat16)
    w = w_ref[0].astype(jnp.bfloat16)
    o_ref[0] = jnp.dot(sc, w, preferred_element_type=jnp.float32) + b_ref[0]


def _adaln_mod(c, w_ada, b_ada):
    depth, _, n_out = w_ada.shape
    batch = c.shape[0]
    n_chunk = n_out // ADALN_COL_CHUNKS
    return pl.pallas_call(
        _adaln_kernel,
        out_shape=jax.ShapeDtypeStruct((depth, batch, n_out), jnp.float32),
        grid=(depth, n_out // n_chunk),
        in_specs=[
            pl.BlockSpec((batch, D_MODEL), lambda i, j: (0, 0)),
            pl.BlockSpec((1, D_MODEL, n_chunk), lambda i, j: (i, 0, j)),
            pl.BlockSpec((1, 1, n_chunk), lambda i, j: (i, 0, j)),
        ],
        out_specs=pl.BlockSpec((1, batch, n_chunk), lambda i, j: (i, 0, j)),
        compiler_params=_params(2),
        name="adaln_mod",
    )(c, w_ada, b_ada.reshape(depth, 1, n_out))


def _attn_pre_kernel(x_ref, mod_ref, gain_ref, w_ref, qg_ref, kg_ref,
                     qt_ref, k_ref, vt_ref, kmean_ref, *, tm):
    mod = mod_ref[0, 0]
    h = _modulated_norm(x_ref[0], gain_ref[...], mod[0:1], mod[1:2]).astype(jnp.bfloat16)
    n_blk = tm // BLOCK
    heads_per_chunk = MXU_COLS // HEAD_DIM
    for chunk in range(3 * D_MODEL // MXU_COLS):
        r = jnp.dot(h, w_ref[:, chunk * MXU_COLS:(chunk + 1) * MXU_COLS],
                    preferred_element_type=jnp.float32)
        kind = (chunk * heads_per_chunk) // N_HEADS
        for half in range(heads_per_chunk):
            head = (chunk * heads_per_chunk) % N_HEADS + half
            t = r[:, half * HEAD_DIM:(half + 1) * HEAD_DIM]
            if kind == 2:
                for blk in range(n_blk):
                    vt_ref[0, head, blk] = t[blk * BLOCK:(blk + 1) * BLOCK].T.astype(jnp.bfloat16)
                continue
            tn = t * lax.rsqrt(jnp.mean(t * t, axis=-1, keepdims=True) + EPS)
            if kind == 0:
                qn = (tn * qg_ref[...]) * Q_SCALE
                for blk in range(n_blk):
                    qt_ref[0, head, blk] = qn[blk * BLOCK:(blk + 1) * BLOCK].T.astype(jnp.bfloat16)
            else:
                kn = tn * kg_ref[...]
                k_ref[0, head] = kn.astype(jnp.bfloat16)
                kmean_ref[0, 0, head] = jnp.mean(kn.reshape(n_blk, BLOCK, HEAD_DIM), axis=1)


def _attn_pre(x, mod, gain, w_qkv_bf16, q_gain, k_gain, *, tm):
    batch, seq, _ = x.shape
    n_tiles = seq // tm
    n_blk = tm // BLOCK
    t_shape = jax.ShapeDtypeStruct((batch, N_HEADS, seq // BLOCK, HEAD_DIM, BLOCK), jnp.bfloat16)
    t_spec = pl.BlockSpec((1, N_HEADS, n_blk, HEAD_DIM, BLOCK), lambda b, t: (b, 0, t, 0, 0))
    return pl.pallas_call(
        functools.partial(_attn_pre_kernel, tm=tm),
        out_shape=(
            t_shape,
            jax.ShapeDtypeStruct((batch, N_HEADS, seq, HEAD_DIM), jnp.bfloat16),
            t_shape,
            jax.ShapeDtypeStruct((batch, n_tiles, N_HEADS, n_blk, HEAD_DIM), jnp.float32),
        ),
        grid=(batch, n_tiles),
        in_specs=[
            pl.BlockSpec((1, tm, D_MODEL), lambda b, t: (b, t, 0)),
            pl.BlockSpec((1, 1, 6, D_MODEL), lambda b, t: (0, b, 0, 0)),
            _const_spec((1, D_MODEL)),
            _const_spec((D_MODEL, 3 * D_MODEL)),
            _const_spec((1, HEAD_DIM)),
            _const_spec((1, HEAD_DIM)),
        ],
        out_specs=(
            t_spec,
            pl.BlockSpec((1, N_HEADS, tm, HEAD_DIM), lambda b, t: (b, 0, t, 0)),
            t_spec,
            pl.BlockSpec((1, 1, N_HEADS, n_blk, HEAD_DIM), lambda b, t: (b, t, 0, 0, 0)),
        ),
        compiler_params=_params(2),
        name="attn_pre",
    )(x, mod, gain, w_qkv_bf16, q_gain, k_gain)


def _moba_kernel(qt_ref, k_blk_ref, vt_blk_ref, kmean_ref, shift_ref, o_ref,
                 k_ref, vt_ref, q2t_ref, stage_ref, acc_ref, *stat_refs, n_blocks, fixed_shift):
    nb = pl.program_id(1)
    for h in range(N_HEADS):
        k_ref[h, nb] = k_blk_ref[0, h, 0]
        vt_ref[h, nb] = vt_blk_ref[0, h, 0]
    blk_id = lax.broadcasted_iota(jnp.int32, (n_blocks, BLOCK), 0).astype(jnp.float32)
    key_pos = lax.broadcasted_iota(jnp.int32, (BLOCK, BLOCK), 0)
    qry_pos = lax.broadcasted_iota(jnp.int32, (BLOCK, BLOCK), 1)
    lane_id = lax.broadcasted_iota(jnp.int32, (BLOCK, LANES), 1)
    neg_inf = jnp.float32(-jnp.inf)
    zero_rows = jnp.zeros((BLOCK - HEAD_DIM - n_blocks - SHIFT_ROWS, BLOCK), jnp.bfloat16)
    ones_rows = jnp.ones((DENOM_ROWS, BLOCK), jnp.bfloat16)
    if not fixed_shift:
        mb_ref, m_ref = stat_refs

    def route_scores(h):
        kmean = kmean_ref[0, h]
        km_hi = kmean.astype(jnp.bfloat16)
        km_lo = (kmean - km_hi.astype(jnp.float32)).astype(jnp.bfloat16)
        r2 = jnp.dot(jnp.concatenate([km_hi, km_lo], axis=0), qt_ref[0, h, 0],
                     preferred_element_type=jnp.float32)
        return r2[:n_blocks] + r2[n_blocks:]

    def route_mask(score):
        score = jnp.where(blk_id < nb.astype(jnp.float32), score, neg_inf)
        mask = jnp.full((n_blocks, BLOCK), MASKED, jnp.float32)
        for _ in range(TOPK_BLOCKS):
            best = jnp.max(score, axis=0, keepdims=True)
            first = jnp.min(jnp.where(score == best, blk_id, float(n_blocks)), axis=0, keepdims=True)
            pick = blk_id == first
            mask = jnp.where(pick & (best > neg_inf), 0.0, mask)
            score = jnp.where(pick, neg_inf, score)
        return mask

    def key_extension(j):
        hit = None if j is None else lane_id == j
        if fixed_shift:
            hit = lane_id == n_blocks if hit is None else hit | (lane_id == n_blocks)
        if hit is None:
            return jnp.zeros((BLOCK, LANES), jnp.bfloat16)
        return jnp.where(hit, 1.0, 0.0).astype(jnp.bfloat16)

    def produce(h, j, extension, slot, own):
        keys = jnp.concatenate([k_ref[h, j], extension], axis=1)
        s = jnp.dot(keys, q2t_ref[h], preferred_element_type=jnp.float32)
        if own:
            s = jnp.where(key_pos <= qry_pos, s, MASKED)
        if fixed_shift:
            stage_ref[slot, h] = jnp.exp2(s).astype(jnp.bfloat16)
        else:
            stage_ref[slot, h] = s
            mb_ref[slot, h] = jnp.max(s, axis=0, keepdims=True)

    def consume(h, j, slot, first):
        values = jnp.concatenate([vt_ref[h, j], ones_rows], axis=0)
        if fixed_shift:
            pv = jnp.dot(values, stage_ref[slot, h], preferred_element_type=jnp.float32)
            acc_ref[h] = pv if first else acc_ref[h] + pv
            return
        m_blk = mb_ref[slot, h]
        if first:
            m_new = m_blk
        else:
            m = m_ref[h]
            m_new = jnp.maximum(m, m_blk)
            alpha = jnp.exp2(m - m_new)
        p = jnp.exp2(stage_ref[slot, h] - m_new).astype(jnp.bfloat16)
        pv = jnp.dot(values, p, preferred_element_type=jnp.float32)
        m_ref[h] = m_new
        acc_ref[h] = pv if first else alpha * acc_ref[h] + pv

    def run_blocks(first_block, count, last):
        for t in range(count):
            j = first_block + t
            look_ahead = not (last and t == count - 1)
            if look_ahead:
                extension = key_extension(j + 1)
            for h in range(N_HEADS):
                if look_ahead:
                    produce(h, j + 1, extension, (t + 1) % 2, False)
                consume(h, j, t % 2, False)

    unrouted = jnp.concatenate([jnp.zeros((n_blocks, BLOCK), jnp.float32), shift_ref[...]],
                               axis=0).astype(jnp.bfloat16)
    for h in range(N_HEADS):
        q2t_ref[h] = jnp.concatenate([qt_ref[0, h, 0], unrouted, zero_rows], axis=0)
    routing_scores = [route_scores(h) for h in range(N_HEADS)]
    extension = key_extension(None)
    for h in range(N_HEADS):
        produce(h, nb, extension, 1, True)
    for h in range(N_HEADS):
        q2t_ref[h, HEAD_DIM:HEAD_DIM + n_blocks, :] = route_mask(routing_scores[h]).astype(jnp.bfloat16)
    extension = key_extension(0)
    for h in range(N_HEADS):
        produce(h, 0, extension, 0, False)
        consume(h, nb, 1, True)

    def block_group(i, _):
        run_blocks(LOOP_BLOCKS * i, LOOP_BLOCKS, False)
        return 0

    n_groups = lax.shift_right_logical(nb, LOOP_BLOCKS.bit_length() - 1)
    group_is_last = jnp.logical_and(jnp.bitwise_and(nb, LOOP_BLOCKS - 1) == 0, n_groups > 0)
    n_plain_groups = n_groups - group_is_last.astype(jnp.int32)
    lax.fori_loop(0, n_plain_groups, block_group, 0)

    @pl.when(group_is_last)
    def _():
        run_blocks(LOOP_BLOCKS * n_plain_groups, LOOP_BLOCKS, True)

    done = n_groups * LOOP_BLOCKS
    count = LOOP_BLOCKS // 2
    while count:
        present = jnp.bitwise_and(nb, count) != 0
        nothing_smaller = jnp.bitwise_and(nb, count - 1) == 0
        for last in ((True, False) if count > 1 else (True,)):
            ends_here = nothing_smaller if last else jnp.logical_not(nothing_smaller)

            @pl.when(jnp.logical_and(present, ends_here))
            def _(first_block=done, count=count, last=last):
                run_blocks(first_block, count, last)
        done = done + jnp.bitwise_and(nb, count)
        count //= 2
    for h in range(N_HEADS):
        out_t = acc_ref[h, 0:HEAD_DIM, :] / acc_ref[h, HEAD_DIM:HEAD_DIM + 1, :]
        o_ref[0, :, h * HEAD_DIM:(h + 1) * HEAD_DIM] = out_t.T.astype(o_ref.dtype)


def _moba_attention(qt, k, vt, kmean, shift_rows, *, fixed_shift):
    batch, n_heads, n_blocks, _, _ = qt.shape
    seq = n_blocks * BLOCK
    k5 = k.reshape(batch, n_heads, n_blocks, BLOCK, HEAD_DIM)
    own_block = lambda b, nb: (b, 0, nb, 0, 0)
    stage_dtype = jnp.bfloat16 if fixed_shift else jnp.float32
    scratch = [
        pltpu.VMEM((n_heads, n_blocks, BLOCK, HEAD_DIM), jnp.bfloat16),
        pltpu.VMEM((n_heads, n_blocks, HEAD_DIM, BLOCK), jnp.bfloat16),
        pltpu.VMEM((n_heads, BLOCK, BLOCK), jnp.bfloat16),
        pltpu.VMEM((2, n_heads, BLOCK, BLOCK), stage_dtype),
        pltpu.VMEM((n_heads, HEAD_DIM + DENOM_ROWS, BLOCK), jnp.float32),
    ]
    if not fixed_shift:
        scratch += [
            pltpu.VMEM((2, n_heads, 1, BLOCK), jnp.float32),
            pltpu.VMEM((n_heads, 1, BLOCK), jnp.float32),
        ]
    return pl.pallas_call(
        functools.partial(_moba_kernel, n_blocks=n_blocks, fixed_shift=fixed_shift),
        out_shape=jax.ShapeDtypeStruct((batch, seq, D_MODEL), jnp.bfloat16),
        grid=(batch, n_blocks),
        in_specs=[
            pl.BlockSpec((1, n_heads, 1, HEAD_DIM, BLOCK), own_block),
            pl.BlockSpec((1, n_heads, 1, BLOCK, HEAD_DIM), own_block),
            pl.BlockSpec((1, n_heads, 1, HEAD_DIM, BLOCK), own_block),
            pl.BlockSpec((1, n_heads, n_blocks, HEAD_DIM), lambda b, nb: (b, 0, 0, 0)),
            _const_spec(shift_rows.shape),
        ],
        out_specs=pl.BlockSpec((1, BLOCK, D_MODEL), lambda b, nb: (b, nb, 0)),
        scratch_shapes=scratch,
        compiler_params=_params(2),
        name="moba_attn_fixed_shift" if fixed_shift else "moba_attn_online",
    )(qt, k5, vt, kmean, shift_rows)


def _attn_post_kernel(x_ref, o_ref, mod_ref, gain_ref, w_o_ref, w_gu_ref, w_down_ref, out_ref):
    mod = mod_ref[0, 0]
    y = jnp.dot(o_ref[0], w_o_ref[...], preferred_element_type=jnp.float32)
    x = x_ref[0] + mod[2:3] * y
    h = _modulated_norm(x, gain_ref[...], mod[3:4], mod[4:5]).astype(jnp.bfloat16)
    out_ref[0] = x + mod[5:6] * _swiglu(h, w_gu_ref, w_down_ref)


def _attn_post(x, o, mod, gain_ffn, w_o, w_gu, w_down, *, tm):
    batch, seq, _ = x.shape
    tile = lambda b, t: (b, t, 0)
    return pl.pallas_call(
        _attn_post_kernel,
        out_shape=jax.ShapeDtypeStruct(x.shape, x.dtype),
        grid=(batch, seq // tm),
        in_specs=[
            pl.BlockSpec((1, tm, D_MODEL), tile),
            pl.BlockSpec((1, tm, D_MODEL), tile),
            pl.BlockSpec((1, 1, 6, D_MODEL), lambda b, t: (0, b, 0, 0)),
            _const_spec((1, D_MODEL)),
            _const_spec(w_o.shape),
            _layer_spec(w_gu.shape, 0),
            _layer_spec(w_down.shape, 0),
        ],
        out_specs=pl.BlockSpec((1, tm, D_MODEL), tile),
        compiler_params=_params(2),
        name="attn_post",
    )(x, o, mod, gain_ffn, w_o, w_gu, w_down)


def _conv_layer_kernel(x_ref, mod_ref, gain_mix_ref, gain_ffn_ref, w_in_ref, conv_w_ref,
                       w_out_ref, w_gu_ref, w_down_ref, out_ref, u_ref, *, tm):
    mod = mod_ref[0, 0]
    h = _modulated_norm(x_ref[0], gain_mix_ref[...], mod[0:1], mod[1:2]).astype(jnp.bfloat16)

    @pl.when(pl.program_id(1) == 0)
    def _():
        u_ref[0:SUBLANES, :] = jnp.zeros((SUBLANES, D_MODEL), jnp.float32)

    @pl.when(pl.program_id(1) > 0)
    def _():
        u_ref[0:SUBLANES, :] = u_ref[tm:tm + SUBLANES, :]

    b_gate = jnp.dot(h, w_in_ref[:, :D_MODEL], preferred_element_type=jnp.float32)
    c_gate = jnp.dot(h, w_in_ref[:, D_MODEL:2 * D_MODEL], preferred_element_type=jnp.float32)
    u = c_gate * jnp.dot(h, w_in_ref[:, 2 * D_MODEL:], preferred_element_type=jnp.float32)
    u_ref[SUBLANES:, :] = u
    y = conv_w_ref[CONV_WIDTH - 1:CONV_WIDTH, :] * u
    for tap in range(CONV_WIDTH - 1):
        back = CONV_WIDTH - 1 - tap
        y = y + conv_w_ref[tap:tap + 1, :] * u_ref[SUBLANES - back:SUBLANES - back + tm, :]
    mix = jnp.dot((b_gate * y).astype(jnp.bfloat16), w_out_ref[...], preferred_element_type=jnp.float32)
    x = x_ref[0] + mod[2:3] * mix
    h = _modulated_norm(x, gain_ffn_ref[...], mod[3:4], mod[4:5]).astype(jnp.bfloat16)
    out_ref[0] = x + mod[5:6] * _swiglu(h, w_gu_ref, w_down_ref)


def _conv_layer(x, mod, gain_mix, gain_ffn, w_in, conv_w, w_out, w_gu, w_down, *, tm):
    batch, seq, _ = x.shape
    tile = lambda b, t: (b, t, 0)
    return pl.pallas_call(
        functools.partial(_conv_layer_kernel, tm=tm),
        out_shape=jax.ShapeDtypeStruct(x.shape, x.dtype),
        grid=(batch, seq // tm),
        in_specs=[
            pl.BlockSpec((1, tm, D_MODEL), tile),
            pl.BlockSpec((1, 1, 6, D_MODEL), lambda b, t: (1, b, 0, 0)),
            _const_spec((1, D_MODEL)),
            _const_spec((1, D_MODEL)),
            _const_spec(w_in.shape),
            _const_spec(conv_w.shape),
            _const_spec(w_out.shape),
            _layer_spec(w_gu.shape, 1),
            _layer_spec(w_down.shape, 1),
        ],
        out_specs=pl.BlockSpec((1, tm, D_MODEL), tile),
        scratch_shapes=[pltpu.VMEM((tm + SUBLANES, D_MODEL), jnp.float32)],
        compiler_params=_params(2),
        name="conv_layer",
    )(x, mod, gain_mix, gain_ffn, w_in, conv_w, w_out, w_gu, w_down)


def kernel(x, c, w_ada, b_ada, norm_mix, norm_ffn, w_qkv, w_o, q_gain, k_gain,
           w_in, conv_w, w_out, w_gate_up, w_down):
    batch, seq, _ = x.shape
    depth = w_ada.shape[0]
    assert depth == 2 and seq % BLOCK == 0
    bf16 = jnp.bfloat16
    mod = _adaln_mod(c, w_ada, b_ada).reshape(depth, batch, 6, D_MODEL)

    tm_pre, tm_post, tm_conv = 1024, 1024, 512
    q, k, vt, kmean = _attn_pre(x, mod, norm_mix[0:1], w_qkv[0].astype(bf16), q_gain[0:1], k_gain[0:1],
                                tm=tm_pre)
    kmean = kmean.transpose(0, 2, 1, 3, 4).reshape(batch, N_HEADS, seq // BLOCK, HEAD_DIM)
    score_bound = HEAD_DIM * Q_SCALE * jnp.max(jnp.abs(q_gain[0])) * jnp.max(jnp.abs(k_gain[0]))
    shift_rows = jnp.zeros((SHIFT_ROWS, BLOCK), jnp.float32).at[0].set(-score_bound)
    o = lax.cond(
        score_bound <= MAX_FIXED_SHIFT,
        functools.partial(_moba_attention, fixed_shift=True),
        functools.partial(_moba_attention, fixed_shift=False),
        q, k, vt, kmean, shift_rows)
    w_gate_up, w_down = w_gate_up.astype(bf16), w_down.astype(bf16)
    x = _attn_post(x, o, mod, norm_ffn[0:1], w_o[0].astype(bf16), w_gate_up, w_down, tm=tm_post)
    x = _conv_layer(x, mod, norm_mix[1:2], norm_ffn[1:2], w_in[0].astype(bf16), conv_w[0],
                    w_out[0].astype(bf16), w_gate_up, w_down, tm=tm_conv)
    return x
```

```python
import functools
import math

import jax
import jax.numpy as jnp
from jax import lax
from jax.experimental import pallas as pl
from jax.experimental.pallas import tpu as pltpu

D_MODEL = 1024
N_HEADS = 8
HEAD_DIM = D_MODEL // N_HEADS
BLOCK = 256
TOPK_BLOCKS = 3
CONV_WIDTH = 3
D_FF = int(math.ceil((8 * D_MODEL / 3) / 256) * 256)
EPS = 1e-6

MXU_COLS = 256
LANES = 128
FF_CHUNK = 2 * MXU_COLS
SUBLANES = 8
DENOM_ROWS = 16
SHIFT_ROWS = 16
LOOP_BLOCKS = 8
ADALN_COL_CHUNKS = 4
MAX_FIXED_SHIFT = 40.0
VMEM_LIMIT_BYTES = 56 * 1024 * 1024

MASKED = -1e30
Q_SCALE = HEAD_DIM ** -0.5 * math.log2(math.e)


def _const_spec(shape):
    zeros = (0,) * len(shape)
    return pl.BlockSpec(shape, lambda *_: zeros, pipeline_mode=pl.Buffered(1))


def _layer_spec(stacked_shape, layer):
    tail = (0,) * (len(stacked_shape) - 1)
    return pl.BlockSpec((1,) + tuple(stacked_shape[1:]), lambda *_: (layer,) + tail,
                        pipeline_mode=pl.Buffered(1))


def _params(n_grid_axes):
    return pltpu.CompilerParams(
        dimension_semantics=("arbitrary",) * n_grid_axes,
        vmem_limit_bytes=VMEM_LIMIT_BYTES,
    )


def _modulated_norm(x, gain, shift, scale):
    y = x * lax.rsqrt(jnp.mean(x * x, axis=-1, keepdims=True) + EPS)
    return y * (gain * (1.0 + scale)) + shift


def _ff_chunks():
    chunks, start = [], 0
    while start < D_FF:
        size = min(FF_CHUNK, D_FF - start)
        chunks.append((start, size))
        start += size
    return chunks


def _swiglu(h_bf16, w_gu_ref, w_down_ref):
    acc = None
    for start, size in _ff_chunks():
        g = jnp.dot(h_bf16, w_gu_ref[0, :, start:start + size], preferred_element_type=jnp.float32)
        u = jnp.dot(h_bf16, w_gu_ref[0, :, D_FF + start:D_FF + start + size],
                    preferred_element_type=jnp.float32)
        act = (g * jax.nn.sigmoid(g) * u).astype(jnp.bfloat16)
        part = jnp.dot(act, w_down_ref[0, start:start + size, :], preferred_element_type=jnp.float32)
        acc = part if acc is None else acc + part
    return acc


def _adaln_kernel(c_ref, w_ref, b_ref, o_ref):
    c = c_ref[...]
    sc = (c * jax.nn.sigmoid(c)).astype(jnp.bfloat16)
    w = w_ref[0].astype(jnp.bfloat16)
    o_ref[0] = jnp.dot(sc, w, preferred_element_type=jnp.float32) + b_ref[0]


def _adaln_mod(c, w_ada, b_ada):
    depth, _, n_out = w_ada.shape
    batch = c.shape[0]
    n_chunk = n_out // ADALN_COL_CHUNKS
    return pl.pallas_call(
        _adaln_kernel,
        out_shape=jax.ShapeDtypeStruct((depth, batch, n_out), jnp.float32),
        grid=(depth, n_out // n_chunk),
        in_specs=[
            pl.BlockSpec((batch, D_MODEL), lambda i, j: (0, 0)),
            pl.BlockSpec((1, D_MODEL, n_chunk), lambda i, j: (i, 0, j)),
            pl.BlockSpec((1, 1, n_chunk), lambda i, j: (i, 0, j)),
        ],
        out_specs=pl.BlockSpec((1, batch, n_chunk), lambda i, j: (i, 0, j)),
        compiler_params=_params(2),
        name="adaln_mod",
    )(c, w_ada, b_ada.reshape(depth, 1, n_out))


def _attn_pre_kernel(x_ref, mod_ref, gain_ref, w_ref, qg_ref, kg_ref,
                     qt_ref, k_ref, vt_ref, kmean_ref, *, tm):
    mod = mod_ref[0, 0]
    h = _modulated_norm(x_ref[0], gain_ref[...], mod[0:1], mod[1:2]).astype(jnp.bfloat16)
    n_blk = tm // BLOCK
    heads_per_chunk = MXU_COLS // HEAD_DIM
    for chunk in range(3 * D_MODEL // MXU_COLS):
        r = jnp.dot(h, w_ref[:, chunk * MXU_COLS:(chunk + 1) * MXU_COLS],
                    preferred_element_type=jnp.float32)
        kind = (chunk * heads_per_chunk) // N_HEADS
        for half in range(heads_per_chunk):
            head = (chunk * heads_per_chunk) % N_HEADS + half
            t = r[:, half * HEAD_DIM:(half + 1) * HEAD_DIM]
            if kind == 2:
                for blk in range(n_blk):
                    vt_ref[0, head, blk] = t[blk * BLOCK:(blk + 1) * BLOCK].T.astype(jnp.bfloat16)
                continue
            tn = t * lax.rsqrt(jnp.mean(t * t, axis=-1, keepdims=True) + EPS)
            if kind == 0:
                qn = (tn * qg_ref[...]) * Q_SCALE
                for blk in range(n_blk):
                    qt_ref[0, head, blk] = qn[blk * BLOCK:(blk + 1) * BLOCK].T.astype(jnp.bfloat16)
            else:
                kn = tn * kg_ref[...]
                k_ref[0, head] = kn.astype(jnp.bfloat16)
                kmean_ref[0, 0, head] = jnp.mean(kn.reshape(n_blk, BLOCK, HEAD_DIM), axis=1)


def _attn_pre(x, mod, gain, w_qkv_bf16, q_gain, k_gain, *, tm):
    batch, seq, _ = x.shape
    n_tiles = seq // tm
    n_blk = tm // BLOCK
    t_shape = jax.ShapeDtypeStruct((batch, N_HEADS, seq // BLOCK, HEAD_DIM, BLOCK), jnp.bfloat16)
    t_spec = pl.BlockSpec((1, N_HEADS, n_blk, HEAD_DIM, BLOCK), lambda b, t: (b, 0, t, 0, 0))
    return pl.pallas_call(
        functools.partial(_attn_pre_kernel, tm=tm),
        out_shape=(
            t_shape,
            jax.ShapeDtypeStruct((batch, N_HEADS, seq, HEAD_DIM), jnp.bfloat16),
            t_shape,
            jax.ShapeDtypeStruct((batch, n_tiles, N_HEADS, n_blk, HEAD_DIM), jnp.float32),
        ),
        grid=(batch, n_tiles),
        in_specs=[
            pl.BlockSpec((1, tm, D_MODEL), lambda b, t: (b, t, 0)),
            pl.BlockSpec((1, 1, 6, D_MODEL), lambda b, t: (0, b, 0, 0)),
            _const_spec((1, D_MODEL)),
            _const_spec((D_MODEL, 3 * D_MODEL)),
            _const_spec((1, HEAD_DIM)),
            _const_spec((1, HEAD_DIM)),
        ],
        out_specs=(
            t_spec,
            pl.BlockSpec((1, N_HEADS, tm, HEAD_DIM), lambda b, t: (b, 0, t, 0)),
            t_spec,
            pl.BlockSpec((1, 1, N_HEADS, n_blk, HEAD_DIM), lambda b, t: (b, t, 0, 0, 0)),
        ),
        compiler_params=_params(2),
        name="attn_pre",
    )(x, mod, gain, w_qkv_bf16, q_gain, k_gain)


def _moba_kernel(qt_ref, k_blk_ref, vt_blk_ref, kmean_ref, shift_ref, o_ref,
                 k_ref, vt_ref, q2t_ref, stage_ref, acc_ref, *stat_refs, n_blocks, fixed_shift):
    nb = pl.program_id(1)
    for h in range(N_HEADS):
        k_ref[h, nb] = k_blk_ref[0, h, 0]
        vt_ref[h, nb] = vt_blk_ref[0, h, 0]
    blk_id = lax.broadcasted_iota(jnp.int32, (n_blocks, BLOCK), 0).astype(jnp.float32)
    key_pos = lax.broadcasted_iota(jnp.int32, (BLOCK, BLOCK), 0)
    qry_pos = lax.broadcasted_iota(jnp.int32, (BLOCK, BLOCK), 1)
    lane_id = lax.broadcasted_iota(jnp.int32, (BLOCK, LANES), 1)
    neg_inf = jnp.float32(-jnp.inf)
    zero_rows = jnp.zeros((BLOCK - HEAD_DIM - n_blocks - SHIFT_ROWS, BLOCK), jnp.bfloat16)
    ones_rows = jnp.ones((DENOM_ROWS, BLOCK), jnp.bfloat16)
    if not fixed_shift:
        mb_ref, m_ref = stat_refs

    def route_scores(h):
        kmean = kmean_ref[0, h]
        km_hi = kmean.astype(jnp.bfloat16)
        km_lo = (kmean - km_hi.astype(jnp.float32)).astype(jnp.bfloat16)
        r2 = jnp.dot(jnp.concatenate([km_hi, km_lo], axis=0), qt_ref[0, h, 0],
                     preferred_element_type=jnp.float32)
        return r2[:n_blocks] + r2[n_blocks:]

    def route_mask(score):
        score = jnp.where(blk_id < nb.astype(jnp.float32), score, neg_inf)
        mask = jnp.full((n_blocks, BLOCK), MASKED, jnp.float32)
        for _ in range(TOPK_BLOCKS):
            best = jnp.max(score, axis=0, keepdims=True)
            first = jnp.min(jnp.where(score == best, blk_id, float(n_blocks)), axis=0, keepdims=True)
            pick = blk_id == first
            mask = jnp.where(pick & (best > neg_inf), 0.0, mask)
            score = jnp.where(pick, neg_inf, score)
        return mask

    def key_extension(j):
        hit = None if j is None else lane_id == j
        if fixed_shift:
            hit = lane_id == n_blocks if hit is None else hit | (lane_id == n_blocks)
        if hit is None:
            return jnp.zeros((BLOCK, LANES), jnp.bfloat16)
        return jnp.where(hit, 1.0, 0.0).astype(jnp.bfloat16)

    def produce(h, j, extension, slot, own):
        keys = jnp.concatenate([k_ref[h, j], extension], axis=1)
        s = jnp.dot(keys, q2t_ref[h], preferred_element_type=jnp.float32)
        if own:
            s = jnp.where(key_pos <= qry_pos, s, MASKED)
        if fixed_shift:
            stage_ref[slot, h] = jnp.exp2(s).astype(jnp.bfloat16)
        else:
            stage_ref[slot, h] = s
            mb_ref[slot, h] = jnp.max(s, axis=0, keepdims=True)

    def consume(h, j, slot, first):
        values = jnp.concatenate([vt_ref[h, j], ones_rows], axis=0)
        if fixed_shift:
            pv = jnp.dot(values, stage_ref[slot, h], preferred_element_type=jnp.float32)
            acc_ref[h] = pv if first else acc_ref[h] + pv
            return
        m_blk = mb_ref[slot, h]
        if first:
            m_new = m_blk
        else:
            m = m_ref[h]
            m_new = jnp.maximum(m, m_blk)
            alpha = jnp.exp2(m - m_new)
        p = jnp.exp2(stage_ref[slot, h] - m_new).astype(jnp.bfloat16)
        pv = jnp.dot(values, p, preferred_element_type=jnp.float32)
        m_ref[h] = m_new
        acc_ref[h] = pv if first else alpha * acc_ref[h] + pv

    def run_blocks(first_block, count, last):
        for t in range(count):
            j = first_block + t
            look_ahead = not (last and t == count - 1)
            if look_ahead:
                extension = key_extension(j + 1)
            for h in range(N_HEADS):
                if look_ahead:
                    produce(h, j + 1, extension, (t + 1) % 2, False)
                consume(h, j, t % 2, False)

    unrouted = jnp.concatenate([jnp.zeros((n_blocks, BLOCK), jnp.float32), shift_ref[...]],
                               axis=0).astype(jnp.bfloat16)
    for h in range(N_HEADS):
        q2t_ref[h] = jnp.concatenate([qt_ref[0, h, 0], unrouted, zero_rows], axis=0)
    routing_scores = [route_scores(h) for h in range(N_HEADS)]
    extension = key_extension(None)
    for h in range(N_HEADS):
        produce(h, nb, extension, 1, True)
    for h in range(N_HEADS):
        q2t_ref[h, HEAD_DIM:HEAD_DIM + n_blocks, :] = route_mask(routing_scores[h]).astype(jnp.bfloat16)
    extension = key_extension(0)
    for h in range(N_HEADS):
        produce(h, 0, extension, 0, False)
        consume(h, nb, 1, True)

    def block_group(i, _):
        run_blocks(LOOP_BLOCKS * i, LOOP_BLOCKS, False)
        return 0

    n_groups = lax.shift_right_logical(nb, LOOP_BLOCKS.bit_length() - 1)
    group_is_last = jnp.logical_and(jnp.bitwise_and(nb, LOOP_BLOCKS - 1) == 0, n_groups > 0)
    n_plain_groups = n_groups - group_is_last.astype(jnp.int32)
    lax.fori_loop(0, n_plain_groups, block_group, 0)

    @pl.when(group_is_last)
    def _():
        run_blocks(LOOP_BLOCKS * n_plain_groups, LOOP_BLOCKS, True)

    done = n_groups * LOOP_BLOCKS
    count = LOOP_BLOCKS // 2
    while count:
        present = jnp.bitwise_and(nb, count) != 0
        nothing_smaller = jnp.bitwise_and(nb, count - 1) == 0
        for last in ((True, False) if count > 1 else (True,)):
            ends_here = nothing_smaller if last else jnp.logical_not(nothing_smaller)

            @pl.when(jnp.logical_and(present, ends_here))
            def _(first_block=done, count=count, last=last):
                run_blocks(first_block, count, last)
        done = done + jnp.bitwise_and(nb, count)
        count //= 2
    for h in range(N_HEADS):
        out_t = acc_ref[h, 0:HEAD_DIM, :] / acc_ref[h, HEAD_DIM:HEAD_DIM + 1, :]
        o_ref[0, h * HEAD_DIM:(h + 1) * HEAD_DIM, :] = out_t.astype(o_ref.dtype)


def _moba_attention(qt, k, vt, kmean, shift_rows, *, fixed_shift):
    batch, n_heads, n_blocks, _, _ = qt.shape
    seq = n_blocks * BLOCK
    k5 = k.reshape(batch, n_heads, n_blocks, BLOCK, HEAD_DIM)
    own_block = lambda b, nb: (b, 0, nb, 0, 0)
    stage_dtype = jnp.bfloat16 if fixed_shift else jnp.float32
    scratch = [
        pltpu.VMEM((n_heads, n_blocks, BLOCK, HEAD_DIM), jnp.bfloat16),
        pltpu.VMEM((n_heads, n_blocks, HEAD_DIM, BLOCK), jnp.bfloat16),
        pltpu.VMEM((n_heads, BLOCK, BLOCK), jnp.bfloat16),
        pltpu.VMEM((2, n_heads, BLOCK, BLOCK), stage_dtype),
        pltpu.VMEM((n_heads, HEAD_DIM + DENOM_ROWS, BLOCK), jnp.float32),
    ]
    if not fixed_shift:
        scratch += [
            pltpu.VMEM((2, n_heads, 1, BLOCK), jnp.float32),
            pltpu.VMEM((n_heads, 1, BLOCK), jnp.float32),
        ]
    return pl.pallas_call(
        functools.partial(_moba_kernel, n_blocks=n_blocks, fixed_shift=fixed_shift),
        out_shape=jax.ShapeDtypeStruct((batch, D_MODEL, seq), jnp.bfloat16),
        grid=(batch, n_blocks),
        in_specs=[
            pl.BlockSpec((1, n_heads, 1, HEAD_DIM, BLOCK), own_block),
            pl.BlockSpec((1, n_heads, 1, BLOCK, HEAD_DIM), own_block),
            pl.BlockSpec((1, n_heads, 1, HEAD_DIM, BLOCK), own_block),
            pl.BlockSpec((1, n_heads, n_blocks, HEAD_DIM), lambda b, nb: (b, 0, 0, 0)),
            _const_spec(shift_rows.shape),
        ],
        out_specs=pl.BlockSpec((1, D_MODEL, BLOCK), lambda b, nb: (b, 0, nb)),
        scratch_shapes=scratch,
        compiler_params=_params(2),
        name="moba_attn_fixed_shift" if fixed_shift else "moba_attn_online",
    )(qt, k5, vt, kmean, shift_rows)


def _attn_post_kernel(x_ref, o_ref, mod_ref, gain_ref, w_o_ref, w_gu_ref, w_down_ref, out_ref):
    mod = mod_ref[0, 0]
    y = lax.dot_general(o_ref[0], w_o_ref[...], (((0,), (0,)), ((), ())),
                        preferred_element_type=jnp.float32)
    x = x_ref[0] + mod[2:3] * y
    h = _modulated_norm(x, gain_ref[...], mod[3:4], mod[4:5]).astype(jnp.bfloat16)
    out_ref[0] = x + mod[5:6] * _swiglu(h, w_gu_ref, w_down_ref)


def _attn_post(x, o, mod, gain_ffn, w_o, w_gu, w_down, *, tm):
    batch, seq, _ = x.shape
    tile = lambda b, t: (b, t, 0)
    return pl.pallas_call(
        _attn_post_kernel,
        out_shape=jax.ShapeDtypeStruct(x.shape, x.dtype),
        grid=(batch, seq // tm),
        in_specs=[
            pl.BlockSpec((1, tm, D_MODEL), tile),
            pl.BlockSpec((1, D_MODEL, tm), lambda b, t: (b, 0, t)),
            pl.BlockSpec((1, 1, 6, D_MODEL), lambda b, t: (0, b, 0, 0)),
            _const_spec((1, D_MODEL)),
            _const_spec(w_o.shape),
            _layer_spec(w_gu.shape, 0),
            _layer_spec(w_down.shape, 0),
        ],
        out_specs=pl.BlockSpec((1, tm, D_MODEL), tile),
        compiler_params=_params(2),
        name="attn_post",
    )(x, o, mod, gain_ffn, w_o, w_gu, w_down)


def _conv_layer_kernel(x_ref, mod_ref, gain_mix_ref, gain_ffn_ref, w_in_ref, conv_w_ref,
                       w_out_ref, w_gu_ref, w_down_ref, out_ref, u_ref, *, tm):
    mod = mod_ref[0, 0]
    h = _modulated_norm(x_ref[0], gain_mix_ref[...], mod[0:1], mod[1:2]).astype(jnp.bfloat16)

    @pl.when(pl.program_id(1) == 0)
    def _():
        u_ref[0:SUBLANES, :] = jnp.zeros((SUBLANES, D_MODEL), jnp.float32)

    @pl.when(pl.program_id(1) > 0)
    def _():
        u_ref[0:SUBLANES, :] = u_ref[tm:tm + SUBLANES, :]

    b_gate = jnp.dot(h, w_in_ref[:, :D_MODEL], preferred_element_type=jnp.float32)
    c_gate = jnp.dot(h, w_in_ref[:, D_MODEL:2 * D_MODEL], preferred_element_type=jnp.float32)
    u = c_gate * jnp.dot(h, w_in_ref[:, 2 * D_MODEL:], preferred_element_type=jnp.float32)
    u_ref[SUBLANES:, :] = u
    y = conv_w_ref[CONV_WIDTH - 1:CONV_WIDTH, :] * u
    for tap in range(CONV_WIDTH - 1):
        back = CONV_WIDTH - 1 - tap
        y = y + conv_w_ref[tap:tap + 1, :] * u_ref[SUBLANES - back:SUBLANES - back + tm, :]
    mix = jnp.dot((b_gate * y).astype(jnp.bfloat16), w_out_ref[...], preferred_element_type=jnp.float32)
    x = x_ref[0] + mod[2:3] * mix
    h = _modulated_norm(x, gain_ffn_ref[...], mod[3:4], mod[4:5]).astype(jnp.bfloat16)
    out_ref[0] = x + mod[5:6] * _swiglu(h, w_gu_ref, w_down_ref)


def _conv_layer(x, mod, gain_mix, gain_ffn, w_in, conv_w, w_out, w_gu, w_down, *, tm):
    batch, seq, _ = x.shape
    tile = lambda b, t: (b, t, 0)
    return pl.pallas_call(
        functools.partial(_conv_layer_kernel, tm=tm),
        out_shape=jax.ShapeDtypeStruct(x.shape, x.dtype),
        grid=(batch, seq // tm),
        in_specs=[
            pl.BlockSpec((1, tm, D_MODEL), tile),
            pl.BlockSpec((1, 1, 6, D_MODEL), lambda b, t: (1, b, 0, 0)),
            _const_spec((1, D_MODEL)),
            _const_spec((1, D_MODEL)),
            _const_spec(w_in.shape),
            _const_spec(conv_w.shape),
            _const_spec(w_out.shape),
            _layer_spec(w_gu.shape, 1),
            _layer_spec(w_down.shape, 1),
        ],
        out_specs=pl.BlockSpec((1, tm, D_MODEL), tile),
        scratch_shapes=[pltpu.VMEM((tm + SUBLANES, D_MODEL), jnp.float32)],
        compiler_params=_params(2),
        name="conv_layer",
    )(x, mod, gain_mix, gain_ffn, w_in, conv_w, w_out, w_gu, w_down)


def kernel(x, c, w_ada, b_ada, norm_mix, norm_ffn, w_qkv, w_o, q_gain, k_gain,
           w_in, conv_w, w_out, w_gate_up, w_down):
    batch, seq, _ = x.shape
    depth = w_ada.shape[0]
    assert depth == 2 and seq % BLOCK == 0
    bf16 = jnp.bfloat16
    mod = _adaln_mod(c, w_ada, b_ada).reshape(depth, batch, 6, D_MODEL)

    tm_pre, tm_post, tm_conv = 1024, 1024, 512
    q, k, vt, kmean = _attn_pre(x, mod, norm_mix[0:1], w_qkv[0].astype(bf16), q_gain[0:1], k_gain[0:1],
                                tm=tm_pre)
    kmean = kmean.transpose(0, 2, 1, 3, 4).reshape(batch, N_HEADS, seq // BLOCK, HEAD_DIM)
    score_bound = HEAD_DIM * Q_SCALE * jnp.max(jnp.abs(q_gain[0])) * jnp.max(jnp.abs(k_gain[0]))
    shift_rows = jnp.zeros((SHIFT_ROWS, BLOCK), jnp.float32).at[0].set(-score_bound)
    o = lax.cond(
        score_bound <= MAX_FIXED_SHIFT,
        functools.partial(_moba_attention, fixed_shift=True),
        functools.partial(_moba_attention, fixed_shift=False),
        q, k, vt, kmean, shift_rows)
    w_gate_up, w_down = w_gate_up.astype(bf16), w_down.astype(bf16)
    x = _attn_post(x, o, mod, norm_ffn[0:1], w_o[0].astype(bf16), w_gate_up, w_down, tm=tm_post)
    x = _conv_layer(x, mod, norm_mix[1:2], norm_ffn[1:2], w_in[0].astype(bf16), conv_w[0],
                    w_out[0].astype(bf16), w_gate_up, w_down, tm=tm_conv)
    return x
```
